```python
import math
import jax, jax.numpy as jnp
from jax import lax
import numpy as np

D_MODEL = 4096
BATCH = 1
SEQ = 8192
DEPTH = 2
DEC_BATCH = 2
DEC_SEQ = 8192
PAST_LEN = 128

HEAD_DIM = 128
DIL_GROUPS = ((128, 1), (512, 4), (2048, 16))
N_GROUPS_A = 3
A_HEADS = 8
A_QKV_COLS = N_GROUPS_A * A_HEADS * HEAD_DIM
A_WIDTH = A_HEADS * HEAD_DIM
B_HEADS = 8
B_QK_DIM = 64
B_V_DIM = 2 * B_QK_DIM
B_QK_COLS = B_HEADS * 2 * B_QK_DIM
B_WIDTH = B_HEADS * B_V_DIM
B_Q_BLOCK = 128
C_WIDTH = 2048
CONV_W = 3
IN_COLS = 3 * A_QKV_COLS + 2 * B_QK_COLS + B_WIDTH + 3 * C_WIDTH
N_BRANCHES = 3
N_ATTN_HEADS = N_GROUPS_A * A_HEADS + B_HEADS
N_EXPERT_GROUPS = 4
EXPERTS_PER_GROUP = 8
N_EXPERTS = N_EXPERT_GROUPS * EXPERTS_PER_GROUP
TOP_K = 2
D_FF_EXPERT = 1024
MOE_BLOCK = 256
RMS_EPS = 1e-6
NEG_BIG = -1e30

kernel_name = 'hybrid_dilated_diffattn_shortconv_hmoe_encoder'


def rms_norm(x, gain):
    xf = x.astype(jnp.float32)
    y = xf * lax.rsqrt(jnp.mean(xf * xf, axis=-1, keepdims=True) + RMS_EPS)
    return (y * gain.astype(jnp.float32)).astype(x.dtype)


def alibi_slopes():
    return 2.0 ** (-8.0 * jnp.arange(1, N_ATTN_HEADS + 1, dtype=jnp.float32) / N_ATTN_HEADS)


def in_split_points():
    sizes = [A_QKV_COLS] * 3 + [B_QK_COLS, B_QK_COLS, B_WIDTH] + [C_WIDTH] * 3
    return [int(v) for v in np.cumsum(sizes)[:-1]]


def dilated_group_attention(q, k, v, dilation, half, slopes):
    bsz, s_len, n_h, dh = q.shape
    u_len = s_len // dilation
    n_blk = -(-u_len // half)
    u_pad = n_blk * half

    def to_blocks(t):
        t = t.reshape(bsz, u_len, dilation, n_h, dh)
        t = jnp.pad(t, ((0, 0), (0, u_pad - u_len), (0, 0), (0, 0), (0, 0)))
        t = t.reshape(bsz, n_blk, half, dilation, n_h, dh)
        return t.transpose(0, 3, 4, 1, 2, 5)

    def neighbours(t):
        tp = jnp.pad(t, ((0, 0), (0, 0), (0, 0), (1, 1), (0, 0), (0, 0)))
        return jnp.concatenate([tp[:, :, :, :-2], tp[:, :, :, 1:-1], tp[:, :, :, 2:]], axis=4)

    qb = to_blocks(q)
    kn = neighbours(to_blocks(k))
    vn = neighbours(to_blocks(v))
    s = jnp.einsum('brhnid,brhnjd->brhnij', qb, kn, preferred_element_type=jnp.float32) * (dh ** -0.5)
    i = jnp.arange(half)[:, None]
    j = jnp.arange(3 * half)[None, :]
    rel = j - half - i
    key_u = (jnp.arange(n_blk)[:, None, None] - 1) * half + j[None]
    valid = (jnp.abs(rel)[None] <= half) & (key_u >= 0) & (key_u < u_len)
    bias = -slopes[:, None, None] * (jnp.abs(rel) * dilation).astype(jnp.float32)[None]
    s = jnp.where(valid[None, None, None], s + bias[None, None, :, None], NEG_BIG)
    lse = jax.nn.logsumexp(s, axis=-1)
    p = jnp.exp(s - lse[..., None])
    o = jnp.einsum('brhnij,brhnjd->brhnid', p.astype(v.dtype), vn)
    o = o.transpose(0, 3, 4, 1, 2, 5).reshape(bsz, u_pad, dilation, n_h, dh)[:, :u_len]
    lse = lse.transpose(0, 3, 4, 1, 2).reshape(bsz, u_pad, dilation, n_h)[:, :u_len]
    return o.reshape(bsz, s_len, n_h, dh), lse.reshape(bsz, s_len, n_h)


def diff_attention(q, k, v, lam, slopes):
    bsz, s_len, n_h, _, dq = q.shape
    n_qb = s_len // B_Q_BLOCK
    qb = q.reshape(bsz, n_qb, B_Q_BLOCK, n_h, 2, dq).transpose(1, 0, 2, 3, 4, 5)
    kpos = jnp.arange(s_len)

    def block(args):
        qi, idx = args
        s = jnp.einsum('bqhcd,bkhcd->bhcqk', qi, k, preferred_element_type=jnp.float32) * (dq ** -0.5)
        qpos = idx * B_Q_BLOCK + jnp.arange(B_Q_BLOCK)
        dist = jnp.abs(qpos[:, None] - kpos[None, :]).astype(jnp.float32)
        s = s - slopes[None, :, None, None, None] * dist
        p = jax.nn.softmax(s, axis=-1)
        a = p[:, :, 0] - lam * p[:, :, 1]
        return jnp.einsum('bhqk,bkhd->bqhd', a.astype(v.dtype), v)

    o = lax.map(block, (qb, jnp.arange(n_qb)))
    return o.transpose(1, 0, 2, 3, 4).reshape(bsz, s_len, n_h, v.shape[-1])


def short_gated_conv(u, b_gate, c_gate, conv_w):
    vp = jnp.pad(c_gate * u, ((0, 0), (1, 1), (0, 0)))
    y = vp[:, :-2] * conv_w[0] + vp[:, 1:-1] * conv_w[1] + vp[:, 2:] * conv_w[2]
    return b_gate * y


def token_mixers(h, layer_idx, slopes, w_in, qnorm_a, knorm_a, qnorm_b, knorm_b, lambda_q1, lambda_k1,
                 lambda_q2, lambda_k2, subln_b, conv_w, w_proj_a, w_proj_b, w_proj_c, w_gate, w_out):
    bsz, s_len, _ = h.shape
    proj = jnp.einsum('bsd,dc->bsc', h, w_in)
    qa, ka, va, qd, kd, vd, cu, cb, cc = jnp.split(proj, in_split_points(), axis=-1)

    qa = rms_norm(qa.reshape(bsz, s_len, N_GROUPS_A, A_HEADS, HEAD_DIM), qnorm_a)
    ka = rms_norm(ka.reshape(bsz, s_len, N_GROUPS_A, A_HEADS, HEAD_DIM), knorm_a)
    va = va.reshape(bsz, s_len, N_GROUPS_A, A_HEADS, HEAD_DIM)
    outs, lses = [], []
    for g, (window, dilation) in enumerate(DIL_GROUPS):
        o, lse = dilated_group_attention(qa[:, :, g], ka[:, :, g], va[:, :, g], dilation,
                                         window // (2 * dilation), slopes[g * A_HEADS:(g + 1) * A_HEADS])
        outs.append(o)
        lses.append(lse)
    mix = jax.nn.softmax(jnp.stack(lses, axis=0), axis=0)
    oa = jnp.sum(mix[..., None].astype(va.dtype) * jnp.stack(outs, axis=0), axis=0).reshape(bsz, s_len, A_WIDTH)

    qd = rms_norm(qd.reshape(bsz, s_len, B_HEADS, 2, B_QK_DIM), qnorm_b)
    kd = rms_norm(kd.reshape(bsz, s_len, B_HEADS, 2, B_QK_DIM), knorm_b)
    vd = vd.reshape(bsz, s_len, B_HEADS, B_V_DIM)
    lam_init = 0.8 - 0.6 * math.exp(-0.3 * layer_idx)
    lam = (jnp.exp(jnp.sum(lambda_q1 * lambda_k1).astype(jnp.float32))
           - jnp.exp(jnp.sum(lambda_q2 * lambda_k2).astype(jnp.float32)) + lam_init)
    ob = diff_attention(qd, kd, vd, lam, slopes[N_GROUPS_A * A_HEADS:])
    ob = (rms_norm(ob, subln_b) * (1.0 - lam_init)).reshape(bsz, s_len, B_WIDTH)

    oc = short_gated_conv(cu, cb, cc, conv_w)

    gates = jax.nn.sigmoid(jnp.einsum('bsd,dc->bsc', h, w_gate)).reshape(bsz, s_len, N_BRANCHES, D_MODEL)
    merged = (gates[:, :, 0] * jnp.einsum('bsc,cd->bsd', oa, w_proj_a)
              + gates[:, :, 1] * jnp.einsum('bsc,cd->bsd', ob, w_proj_b)
              + gates[:, :, 2] * jnp.einsum('bsc,cd->bsd', oc, w_proj_c))
    return jnp.einsum('bsd,de->bse', merged, w_out)


def hierarchical_moe(h, w_route_group, w_route_expert, w_gate_up, w_down):
    bsz, s_len, d = h.shape
    n_tok = bsz * s_len
    x = h.reshape(n_tok, d)
    g_prob = jax.nn.softmax(jnp.einsum('nd,dg->ng', x, w_route_group, preferred_element_type=jnp.float32), axis=-1)
    g_w, g_idx = lax.top_k(g_prob, 1)
    e_logits = jnp.einsum('nd,de->ne', x, w_route_expert, preferred_element_type=jnp.float32)
    e_logits = e_logits.reshape(n_tok, N_EXPERT_GROUPS, EXPERTS_PER_GROUP)[jnp.arange(n_tok), g_idx[:, 0]]
    e_w, e_idx = lax.top_k(jax.nn.softmax(e_logits, axis=-1), TOP_K)
    weights = g_w * e_w / jnp.sum(e_w, axis=-1, keepdims=True)
    expert_ids = g_idx * EXPERTS_PER_GROUP + e_idx

    nk = n_tok * TOP_K
    flat_e = expert_ids.reshape(nk).astype(jnp.int32)
    flat_w = weights.reshape(nk)
    flat_tok = jnp.arange(nk, dtype=jnp.int32) // TOP_K
    order = jnp.argsort(flat_e)
    se, stok, sw = flat_e[order], flat_tok[order], flat_w[order]
    counts = jnp.zeros((N_EXPERTS,), jnp.int32).at[flat_e].add(1)
    pcounts = (counts + MOE_BLOCK - 1) // MOE_BLOCK * MOE_BLOCK
    pend = jnp.cumsum(pcounts)
    pstart = pend - pcounts
    start = jnp.cumsum(counts) - counts
    dest = pstart[se] + jnp.arange(nk, dtype=jnp.int32) - start[se]
    n_blocks = -(-nk // MOE_BLOCK) + N_EXPERTS
    n_slots = n_blocks * MOE_BLOCK
    slot_tok = jnp.full((n_slots,), n_tok, jnp.int32).at[dest].set(stok)
    slot_w = jnp.zeros((n_slots,), jnp.float32).at[dest].set(sw)
    block_exp = jnp.minimum(jnp.searchsorted(pend, jnp.arange(n_blocks) * MOE_BLOCK, side='right'), N_EXPERTS - 1)
    x_pad = jnp.concatenate([x, jnp.zeros((1, d), x.dtype)], axis=0)

    def body(acc, blk):
        tok = lax.dynamic_slice(slot_tok, (blk * MOE_BLOCK,), (MOE_BLOCK,))
        wt = lax.dynamic_slice(slot_w, (blk * MOE_BLOCK,), (MOE_BLOCK,))
        e = block_exp[blk]
        gu = x_pad[tok] @ w_gate_up[e]
        yb = (jax.nn.silu(gu[:, :D_FF_EXPERT]) * gu[:, D_FF_EXPERT:]) @ w_down[e]
        return acc.at[tok].add(yb * wt[:, None].astype(yb.dtype)), None

    acc, _ = lax.scan(body, jnp.zeros((n_tok + 1, d), x.dtype), jnp.arange(n_blocks))
    return acc[:n_tok].reshape(bsz, s_len, d)


def encoder_trunk(x, norm_mix, w_in, qnorm_a, knorm_a, qnorm_b, knorm_b, lambda_q1, lambda_k1, lambda_q2,
                  lambda_k2, subln_b, conv_w, w_proj_a, w_proj_b, w_proj_c, w_gate, w_out, norm_ffn,
                  w_route_group, w_route_expert, w_gate_up, w_down):
    slopes = alibi_slopes()
    for l in range(DEPTH):
        h = rms_norm(x, norm_mix[l])
        x = x + token_mixers(h, l, slopes, w_in[l], qnorm_a[l], knorm_a[l], qnorm_b[l], knorm_b[l],
                             lambda_q1[l], lambda_k1[l], lambda_q2[l], lambda_k2[l], subln_b[l], conv_w[l],
                             w_proj_a[l], w_proj_b[l], w_proj_c[l], w_gate[l], w_out[l])
        h = rms_norm(x, norm_ffn[l])
        x = x + hierarchical_moe(h, w_route_group[l], w_route_expert[l], w_gate_up[l], w_down[l])
    return x


def setup_inputs(seed: int = 0) -> dict:
    key = jax.random.key(seed)
    ks = jax.random.split(key, 24)

    def nrm(k, shape, scale):
        return jax.random.normal(k, shape, jnp.float32) * scale

    def gain(k, shape):
        return 1.0 + 0.02 * jax.random.normal(k, shape, jnp.float32)

    return {
        'x_prompt': nrm(ks[0], (BATCH, SEQ, D_MODEL), 1.0),
        'x_sample': nrm(ks[1], (DEC_BATCH, DEC_SEQ, D_MODEL), 1.0),
        'norm_mix': gain(ks[2], (DEPTH, D_MODEL)),
        'w_in': nrm(ks[3], (DEPTH, D_MODEL, IN_COLS), D_MODEL ** -0.5),
        'qnorm_a': gain(ks[4], (DEPTH, HEAD_DIM)),
        'knorm_a': gain(ks[5], (DEPTH, HEAD_DIM)),
        'qnorm_b': gain(ks[6], (DEPTH, B_QK_DIM)),
        'knorm_b': gain(ks[7], (DEPTH, B_QK_DIM)),
        'lambda_q1': nrm(ks[8], (DEPTH, B_QK_DIM), 0.1),
        'lambda_k1': nrm(ks[9], (DEPTH, B_QK_DIM), 0.1),
        'lambda_q2': nrm(ks[10], (DEPTH, B_QK_DIM), 0.1),
        'lambda_k2': nrm(ks[11], (DEPTH, B_QK_DIM), 0.1),
        'subln_b': gain(ks[12], (DEPTH, B_V_DIM)),
        'conv_w': nrm(ks[13], (DEPTH, CONV_W, C_WIDTH), CONV_W ** -0.5),
        'w_proj_a': nrm(ks[14], (DEPTH, A_WIDTH, D_MODEL), A_WIDTH ** -0.5),
        'w_proj_b': nrm(ks[15], (DEPTH, B_WIDTH, D_MODEL), B_WIDTH ** -0.5),
        'w_proj_c': nrm(ks[16], (DEPTH, C_WIDTH, D_MODEL), C_WIDTH ** -0.5),
        'w_gate': nrm(ks[17], (DEPTH, D_MODEL, N_BRANCHES * D_MODEL), D_MODEL ** -0.5),
        'w_out': nrm(ks[18], (DEPTH, D_MODEL, D_MODEL), D_MODEL ** -0.5),
        'norm_ffn': gain(ks[19], (DEPTH, D_MODEL)),
        'w_route_group': nrm(ks[20], (DEPTH, D_MODEL, N_EXPERT_GROUPS), D_MODEL ** -0.5),
        'w_route_expert': nrm(ks[21], (DEPTH, D_MODEL, N_EXPERTS), D_MODEL ** -0.5),
        'w_gate_up': nrm(ks[22], (DEPTH, N_EXPERTS, D_MODEL, 2 * D_FF_EXPERT), D_MODEL ** -0.5),
        'w_down': nrm(ks[23], (DEPTH, N_EXPERTS, D_FF_EXPERT, D_MODEL), D_FF_EXPERT ** -0.5),
    }


def reference(x_prompt, x_sample, norm_mix, w_in, qnorm_a, knorm_a, qnorm_b, knorm_b, lambda_q1, lambda_k1,
              lambda_q2, lambda_k2, subln_b, conv_w, w_proj_a, w_proj_b, w_proj_c, w_gate, w_out, norm_ffn,
              w_route_group, w_route_expert, w_gate_up, w_down):
    y_prompt = encoder_trunk(x_prompt, norm_mix, w_in, qnorm_a, knorm_a, qnorm_b, knorm_b, lambda_q1, lambda_k1,
                             lambda_q2, lambda_k2, subln_b, conv_w, w_proj_a, w_proj_b, w_proj_c, w_gate, w_out,
                             norm_ffn, w_route_group, w_route_expert, w_gate_up, w_down)
    y_sample = encoder_trunk(x_sample, norm_mix, w_in, qnorm_a, knorm_a, qnorm_b, knorm_b, lambda_q1, lambda_k1,
                             lambda_q2, lambda_k2, subln_b, conv_w, w_proj_a, w_proj_b, w_proj_c, w_gate, w_out,
                             norm_ffn, w_route_group, w_route_expert, w_gate_up, w_down)
    return (y_prompt, y_sample)
```

```python
import functools
import math

import jax
import jax.numpy as jnp
from jax import lax
from jax.experimental import pallas as pl
from jax.experimental.pallas import tpu as pltpu

HEAD_DIM = 128
DIL_GROUPS = ((128, 1), (512, 4), (2048, 16))
A_HEADS = 8
B_HEADS = 8
B_QK_DIM = 64
C_WIDTH = 2048
N_EXPERT_GROUPS = 4
EXPERTS_PER_GROUP = 8
D_FF_EXPERT = 1024
MOE_BLOCK = 256
RMS_EPS = 1e-6
NEG_BIG = -1e30

LANES = 128
V7X_VMEM_LIMIT_BYTES = 56 * 1024 * 1024

A_HALF = DIL_GROUPS[0][0] // (2 * DIL_GROUPS[0][1])


def _cparams(n_grid_dims, vmem_bytes):
    return pltpu.CompilerParams(
        dimension_semantics=("arbitrary",) * n_grid_dims,
        vmem_limit_bytes=int(min(max(vmem_bytes, 16 * 1024 * 1024), V7X_VMEM_LIMIT_BYTES)),
    )


def _tile(n, pref):
    t = min(n, pref)
    while n % t:
        t //= 2
    return t


def _rmsnorm_kernel(x_ref, g_ref, o_ref):
    x = x_ref[...]
    ms = jnp.mean(x * x, axis=-1, keepdims=True)
    o_ref[...] = (x * lax.rsqrt(ms + RMS_EPS) * g_ref[...]).astype(o_ref.dtype)


def _rmsnorm(x, gain):
    n, d = x.shape
    tm = _tile(n, 256)
    return pl.pallas_call(
        _rmsnorm_kernel,
        out_shape=jax.ShapeDtypeStruct((n, d), jnp.bfloat16),
        grid=(n // tm,),
        in_specs=[pl.BlockSpec((tm, d), lambda i: (i, 0)),
                  pl.BlockSpec((1, d), lambda i: (0, 0))],
        out_specs=pl.BlockSpec((tm, d), lambda i: (i, 0)),
        compiler_params=_cparams(1, 4 * tm * d * 4),
        name="rmsnorm",
    )(x, gain.reshape(1, d))


def _matmul_kernel(*refs, sigmoid, has_res):
    if has_res:
        x_ref, w_ref, r_ref, o_ref = refs
    else:
        x_ref, w_ref, o_ref = refs
    acc = jnp.dot(x_ref[...], w_ref[...], preferred_element_type=jnp.float32)
    if sigmoid:
        acc = 1.0 / (1.0 + jnp.exp(-acc))
    if has_res:
        acc = acc + r_ref[...]
    o_ref[...] = acc.astype(o_ref.dtype)


def _matmul(x, w, out_dtype, *, sigmoid=False, residual=None, tm=1024, tn=1024, name="matmul"):
    n, k = x.shape
    c = w.shape[1]
    tm, tn = _tile(n, tm), _tile(c, tn)
    in_specs = [pl.BlockSpec((tm, k), lambda j, i: (i, 0)),
                pl.BlockSpec((k, tn), lambda j, i: (0, j))]
    args = [x, w]
    out_bytes = jnp.dtype(out_dtype).itemsize
    vmem = 2 * (tm * k * 2 + k * tn * 2 + tm * tn * out_bytes) + 2 * tm * tn * 4
    if residual is not None:
        in_specs.append(pl.BlockSpec((tm, tn), lambda j, i: (i, j)))
        args.append(residual)
        vmem += 2 * tm * tn * 4
    return pl.pallas_call(
        functools.partial(_matmul_kernel, sigmoid=sigmoid, has_res=residual is not None),
        out_shape=jax.ShapeDtypeStruct((n, c), out_dtype),
        grid=(c // tn, n // tm),
        in_specs=in_specs,
        out_specs=pl.BlockSpec((tm, tn), lambda j, i: (i, j)),
        compiler_params=_cparams(2, vmem + (4 << 20)),
        name=name,
    )(*args)


def _head_rmsnorm(x, gain):
    ms = jnp.mean(x * x, axis=-1, keepdims=True)
    return x * lax.rsqrt(ms + RMS_EPS) * gain


def _attn_a_kernel(q_ref, kp_ref, km_ref, kn_ref, vp_ref, vm_ref, vn_ref, gq_ref, gk_ref,
                   o_ref, lse_ref, qn_scr, kn_scr, v_scr, *, dilation, slopes, tu, sub):
    half = A_HALF
    i = pl.program_id(2)
    n_i = pl.num_programs(2)
    n_heads = len(slopes)
    scale = HEAD_DIM ** -0.5

    for h in range(n_heads):
        cs = slice(h * HEAD_DIM, (h + 1) * HEAD_DIM)
        qn_scr[:, cs] = _head_rmsnorm(q_ref[:, cs].astype(jnp.float32), gq_ref[...]).astype(qn_scr.dtype)
        for off, ref, rows in ((0, kp_ref, half), (half, km_ref, tu), (half + tu, kn_ref, half)):
            kn_scr[off:off + rows, cs] = _head_rmsnorm(ref[:, cs].astype(jnp.float32),
                                                       gk_ref[...]).astype(kn_scr.dtype)
    v_scr[0:half, :] = vp_ref[...]
    v_scr[half:half + tu, :] = vm_ref[...]
    v_scr[half + tu:, :] = vn_ref[...]

    kw = sub + 2 * half
    ii = lax.broadcasted_iota(jnp.int32, (sub, kw), 0)
    jj = lax.broadcasted_iota(jnp.int32, (sub, kw), 1)
    rel = jnp.abs(jj - half - ii)
    band = rel <= half
    dist = (rel * dilation).astype(jnp.float32)
    lane = lax.broadcasted_iota(jnp.int32, (sub, LANES), 1)

    def body(t, carry):
        a = pl.multiple_of(t * sub, sub)
        kpos = a + jj
        valid = band & ((kpos >= half) | (i > 0)) & ((kpos < tu + half) | (i < n_i - 1))
        lse_tile = jnp.zeros((sub, LANES), jnp.float32)
        for h in range(n_heads):
            cs = slice(h * HEAD_DIM, (h + 1) * HEAD_DIM)
            qh = qn_scr[pl.ds(a, sub), cs]
            kh = kn_scr[pl.ds(a, kw), cs]
            vh = v_scr[pl.ds(a, kw), cs]
            s = lax.dot_general(qh, kh, (((1,), (1,)), ((), ())), preferred_element_type=jnp.float32)
            s = jnp.where(valid, s * scale - slopes[h] * dist, NEG_BIG)
            m = jnp.max(s, axis=-1, keepdims=True)
            p = jnp.exp(s - m)
            l = jnp.sum(p, axis=-1, keepdims=True)
            o = jnp.dot(p.astype(vh.dtype), vh, preferred_element_type=jnp.float32) / l
            o_ref[pl.ds(a, sub), cs] = o.astype(o_ref.dtype)
            lse_tile = jnp.where(lane == h, m + jnp.log(l), lse_tile)
        lse_ref[pl.ds(a, sub), :] = lse_tile
        return carry

    lax.fori_loop(0, tu // sub, body, 0)


def _attn_a_group(proj3, gq, gk, *, group, dilation, slopes, in_cols, n_heads):
    n_seq, s_len, _ = proj3.shape
    half = A_HALF
    u_len = s_len // dilation
    width = n_heads * HEAD_DIM
    a_cols = len(DIL_GROUPS) * width
    tu = _tile(u_len, 512)
    sub = min(tu, 128)
    pv = proj3.reshape(n_seq, u_len, dilation * in_cols)
    blocks_per_row = in_cols // width
    q_blk = group
    k_blk = a_cols // width + group
    v_blk = 2 * a_cols // width + group
    nh = tu // half
    n_halo = u_len // half

    def main(col):
        return pl.BlockSpec((None, tu, width), lambda b, r, i: (b, i, r * blocks_per_row + col))

    def prev(col):
        return pl.BlockSpec((None, half, width),
                            lambda b, r, i: (b, jnp.maximum(i * nh - 1, 0), r * blocks_per_row + col))

    def nxt(col):
        return pl.BlockSpec((None, half, width),
                            lambda b, r, i: (b, jnp.minimum((i + 1) * nh, n_halo - 1), r * blocks_per_row + col))

    gspec = pl.BlockSpec((1, HEAD_DIM), lambda b, r, i: (0, 0))
    o, lse = pl.pallas_call(
        functools.partial(_attn_a_kernel, dilation=dilation, slopes=slopes, tu=tu, sub=sub),
        out_shape=(jax.ShapeDtypeStruct((n_seq, u_len, dilation * width), jnp.bfloat16),
                   jax.ShapeDtypeStruct((n_seq, u_len, dilation * LANES), jnp.float32)),
        grid=(n_seq, dilation, u_len // tu),
        in_specs=[main(q_blk), prev(k_blk), main(k_blk), nxt(k_blk),
                  prev(v_blk), main(v_blk), nxt(v_blk), gspec, gspec],
        out_specs=(pl.BlockSpec((None, tu, width), lambda b, r, i: (b, i, r)),
                   pl.BlockSpec((None, tu, LANES), lambda b, r, i: (b, i, r))),
        scratch_shapes=[pltpu.VMEM((tu, width), jnp.bfloat16),
                        pltpu.VMEM((tu + 2 * half, width), jnp.bfloat16),
                        pltpu.VMEM((tu + 2 * half, width), jnp.bfloat16)],
        compiler_params=_cparams(3, 32 << 20),
        name=f"attn_a_g{group}",
    )(pv, pv, pv, pv, pv, pv, pv, gq.reshape(1, HEAD_DIM), gk.reshape(1, HEAD_DIM))
    return o.reshape(n_seq, s_len, width), lse.reshape(n_seq, s_len, LANES)


def _combine_a_kernel(o0_ref, o1_ref, o2_ref, l0_ref, l1_ref, l2_ref, out_ref, *, n_heads):
    l0, l1, l2 = l0_ref[...], l1_ref[...], l2_ref[...]
    m = jnp.maximum(jnp.maximum(l0, l1), l2)
    e0, e1, e2 = jnp.exp(l0 - m), jnp.exp(l1 - m), jnp.exp(l2 - m)
    den = e0 + e1 + e2
    w0, w1, w2 = e0 / den, e1 / den, e2 / den
    for h in range(n_heads):
        cs = slice(h * HEAD_DIM, (h + 1) * HEAD_DIM)
        acc = (w0[:, h:h + 1] * o0_ref[:, cs].astype(jnp.float32)
               + w1[:, h:h + 1] * o1_ref[:, cs].astype(jnp.float32)
               + w2[:, h:h + 1] * o2_ref[:, cs].astype(jnp.float32))
        out_ref[:, cs] = acc.astype(out_ref.dtype)


def _combine_a(outs, lses, n_heads):
    n, width = outs[0].shape
    tm = _tile(n, 512)
    ospec = pl.BlockSpec((tm, width), lambda i: (i, 0))
    lspec = pl.BlockSpec((tm, LANES), lambda i: (i, 0))
    return pl.pallas_call(
        functools.partial(_combine_a_kernel, n_heads=n_heads),
        out_shape=jax.ShapeDtypeStruct((n, width), jnp.bfloat16),
        grid=(n // tm,),
        in_specs=[ospec, ospec, ospec, lspec, lspec, lspec],
        out_specs=ospec,
        compiler_params=_cparams(1, 24 << 20),
        name="combine_a",
    )(*outs, *lses)


_B_FEAT = 4


def _split_hi_lo(x):
    hi = x.astype(jnp.bfloat16)
    lo = (x - hi.astype(jnp.float32)).astype(jnp.bfloat16)
    return hi.astype(jnp.float32), lo.astype(jnp.float32)


def _attn_b_kernel(q_ref, k_ref, v_ref, gq_ref, gk_ref, lam_ref, gs_ref, slope_ref, o_ref,
                   k_scr, q_scr, dist_scr, m_scr, l_scr, acc_scr, *, lam_init, tq, tk, s_len):
    i = pl.program_id(2)
    dq = B_QK_DIM
    n_kt = s_len // tk
    slope = slope_ref[...]
    slope1 = slope[:, 0:1]

    lane_q = lax.broadcasted_iota(jnp.int32, (tq, 2 * dq), 1)
    lane_k = lax.broadcasted_iota(jnp.int32, (tk, 2 * dq), 1)

    def norm_maps(x, gain, lane):
        in0 = lane < dq
        sq = x * x
        ms0 = jnp.sum(jnp.where(in0, sq, 0.0), axis=-1, keepdims=True) / dq
        ms1 = jnp.sum(jnp.where(in0, 0.0, sq), axis=-1, keepdims=True) / dq
        inv = jnp.where(in0, lax.rsqrt(ms0 + RMS_EPS), lax.rsqrt(ms1 + RMS_EPS))
        return x * inv * gain

    @pl.when(i == 0)
    def _():
        def kbody(t, carry):
            r0 = pl.multiple_of(t * tk, tk)
            kn = norm_maps(k_ref[pl.ds(r0, tk), :].astype(jnp.float32), gk_ref[...], lane_k)
            b = lax.broadcasted_iota(jnp.int32, (tk, 2 * dq), 0).astype(jnp.float32)
            l_hi, l_lo = _split_hi_lo(slope * b)
            r_hi, r_lo = _split_hi_lo(slope * (tk - 1 - b))
            for m in range(2):
                f0 = (1 - m) * dq
                feat = jnp.where(lane_k == f0, l_hi,
                       jnp.where(lane_k == f0 + 1, l_lo,
                       jnp.where(lane_k == f0 + 2, r_hi,
                       jnp.where(lane_k == f0 + 3, r_lo, 0.0))))
                own = (lane_k >= m * dq) & (lane_k < (m + 1) * dq)
                k_scr[m, pl.ds(r0, tk), :] = jnp.where(own, kn, feat).astype(k_scr.dtype)
            return carry
        lax.fori_loop(0, n_kt, kbody, 0)
        a = lax.broadcasted_iota(jnp.int32, (tq, tk), 0)
        b = lax.broadcasted_iota(jnp.int32, (tq, tk), 1)
        dist_scr[...] = jnp.abs(a - b).astype(jnp.float32)

    qn = norm_maps(q_ref[...].astype(jnp.float32), gq_ref[...], lane_q) * (dq ** -0.5)
    for m in range(2):
        f0 = (1 - m) * dq
        own = (lane_q >= m * dq) & (lane_q < (m + 1) * dq)
        left = (lane_q == f0) | (lane_q == f0 + 1)
        right = (lane_q == f0 + 2) | (lane_q == f0 + 3)
        q_scr[m, 0] = jnp.where(own, qn, 0.0).astype(q_scr.dtype)
        q_scr[m, 1] = jnp.where(own, qn, jnp.where(left, 1.0, 0.0)).astype(q_scr.dtype)
        q_scr[m, 2] = jnp.where(own, qn, jnp.where(right, 1.0, 0.0)).astype(q_scr.dtype)

    m_scr[...] = jnp.full(m_scr.shape, NEG_BIG, jnp.float32)
    l_scr[...] = jnp.zeros(l_scr.shape, jnp.float32)
    acc_scr[...] = jnp.zeros(acc_scr.shape, jnp.float32)

    q_pos = i * tq + lax.broadcasted_iota(jnp.int32, (tq, 1), 0)

    def tile(j, ver):
        r0 = pl.multiple_of(j * tk, tk)
        vt = v_ref[pl.ds(r0, tk), :]
        j0 = j * tk
        for m in range(2):
            kt = k_scr[m, pl.ds(r0, tk), :]
            s = lax.dot_general(q_scr[m, ver], kt, (((1,), (1,)), ((), ())),
                                preferred_element_type=jnp.float32)
            if ver == 0:
                s = s - slope1 * dist_scr[...]
                shift = jnp.zeros((tq, 1), jnp.float32)
            elif ver == 1:
                shift = slope1 * (j0 - q_pos).astype(jnp.float32)
            else:
                shift = slope1 * (q_pos - j0 - (tk - 1)).astype(jnp.float32)
            m_old = m_scr[m]
            m_new = jnp.maximum(m_old, jnp.max(s, axis=-1, keepdims=True) + shift)
            p = jnp.exp(s - (m_new - shift))
            alpha = jnp.exp(m_old - m_new)
            l_scr[m] = alpha * l_scr[m] + jnp.sum(p, axis=-1, keepdims=True)
            acc_scr[m] = alpha * acc_scr[m] + jnp.dot(p.astype(vt.dtype), vt,
                                                      preferred_element_type=jnp.float32)
            m_scr[m] = m_new

    i_kt = i * (tq // tk)
    tile(i_kt, 0)

    def left_body(j, carry):
        tile(j, 1)
        return carry

    def right_body(j, carry):
        tile(j, 2)
        return carry

    lax.fori_loop(0, i_kt, left_body, 0)
    lax.fori_loop(i_kt + 1, n_kt, right_body, 0)

    lam_v = lam_ref[...]
    lam = (jnp.exp(jnp.sum(lam_v[0:1] * lam_v[1:2], axis=-1, keepdims=True))
           - jnp.exp(jnp.sum(lam_v[2:3] * lam_v[3:4], axis=-1, keepdims=True)) + lam_init)
    out = acc_scr[0] / l_scr[0] - lam * (acc_scr[1] / l_scr[1])
    out = _head_rmsnorm(out, gs_ref[...]) * (1.0 - lam_init)
    o_ref[...] = out.astype(o_ref.dtype)


def _attn_b(proj3, gq, gk, lam_vecs, subln, slopes, *, lam_init, q_col, k_col, v_col, n_heads):
    n_seq, s_len, _ = proj3.shape
    hw = 2 * B_QK_DIM
    tq = tk = _tile(s_len, 512)
    qb, kb, vb = q_col // hw, k_col // hw, v_col // hw
    gq2 = jnp.tile(gq, 2).reshape(1, hw)
    gk2 = jnp.tile(gk, 2).reshape(1, hw)
    slope_arr = jnp.broadcast_to(jnp.asarray(slopes, jnp.float32)[:, None, None], (n_heads, 1, LANES))
    small = lambda shape: pl.BlockSpec(shape, lambda b, h, i: (0, 0))
    return pl.pallas_call(
        functools.partial(_attn_b_kernel, lam_init=lam_init, tq=tq, tk=tk, s_len=s_len),
        out_shape=jax.ShapeDtypeStruct((n_seq, s_len, n_heads * hw), jnp.bfloat16),
        grid=(n_seq, n_heads, s_len // tq),
        in_specs=[pl.BlockSpec((None, tq, hw), lambda b, h, i: (b, i, qb + h)),
                  pl.BlockSpec((None, s_len, hw), lambda b, h, i: (b, 0, kb + h)),
                  pl.BlockSpec((None, s_len, hw), lambda b, h, i: (b, 0, vb + h)),
                  small((1, hw)), small((1, hw)), small((4, B_QK_DIM)), small((1, hw)),
                  pl.BlockSpec((None, 1, LANES), lambda b, h, i: (h, 0, 0))],
        out_specs=pl.BlockSpec((None, tq, hw), lambda b, h, i: (b, i, h)),
        scratch_shapes=[pltpu.VMEM((2, s_len, hw), jnp.bfloat16),
                        pltpu.VMEM((2, 3, tq, hw), jnp.bfloat16),
                        pltpu.VMEM((tq, tk), jnp.float32),
                        pltpu.VMEM((2, tq, 1), jnp.float32),
                        pltpu.VMEM((2, tq, 1), jnp.float32),
                        pltpu.VMEM((2, tq, hw), jnp.float32)],
        compiler_params=_cparams(3, 40 << 20),
        name="attn_b",
    )(proj3, proj3, proj3, gq2, gk2, lam_vecs, subln.reshape(1, hw), slope_arr)


_CONV_HALO = 16


def _conv_kernel(u_ref, b_ref, c_ref, up_ref, cp_ref, un_ref, cn_ref, w_ref, o_ref, *, tm):
    i = pl.program_id(1)
    n_i = pl.num_programs(1)
    f32 = jnp.float32
    v = c_ref[...].astype(f32) * u_ref[...].astype(f32)
    v_prev = (cp_ref[_CONV_HALO - 1:_CONV_HALO, :].astype(f32) * up_ref[_CONV_HALO - 1:_CONV_HALO, :].astype(f32))
    v_next = cn_ref[0:1, :].astype(f32) * un_ref[0:1, :].astype(f32)
    v_prev = jnp.where(i > 0, v_prev, 0.0)
    v_next = jnp.where(i < n_i - 1, v_next, 0.0)
    row = lax.broadcasted_iota(jnp.int32, v.shape, 0)
    down = jnp.where(row == 0, v_prev, pltpu.roll(v, 1, axis=0))
    up = jnp.where(row == tm - 1, v_next, pltpu.roll(v, tm - 1, axis=0))
    w = w_ref[...]
    y = down * w[0:1] + v * w[1:2] + up * w[2:3]
    o_ref[...] = (b_ref[...].astype(f32) * y).astype(o_ref.dtype)


def _conv(proj3, conv_w, *, u_col, b_col, c_col):
    n_seq, s_len, _ = proj3.shape
    cw = conv_w.shape[1]
    tm = _tile(s_len, 512)
    tc = _tile(cw, 1024)
    ncb = cw // tc
    nh = tm // _CONV_HALO
    n_halo = s_len // _CONV_HALO

    def main(col):
        return pl.BlockSpec((None, tm, tc), lambda b, i, c: (b, i, col // tc + c))

    def prev(col):
        return pl.BlockSpec((None, _CONV_HALO, tc),
                            lambda b, i, c: (b, jnp.maximum(i * nh - 1, 0), col // tc + c))

    def nxt(col):
        return pl.BlockSpec((None, _CONV_HALO, tc),
                            lambda b, i, c: (b, jnp.minimum((i + 1) * nh, n_halo - 1), col // tc + c))

    return pl.pallas_call(
        functools.partial(_conv_kernel, tm=tm),
        out_shape=jax.ShapeDtypeStruct((n_seq, s_len, cw), jnp.bfloat16),
        grid=(n_seq, s_len // tm, ncb),
        in_specs=[main(u_col), main(b_col), main(c_col), prev(u_col), prev(c_col), nxt(u_col), nxt(c_col),
                  pl.BlockSpec((3, tc), lambda b, i, c: (0, c))],
        out_specs=pl.BlockSpec((None, tm, tc), lambda b, i, c: (b, i, c)),
        compiler_params=_cparams(3, 32 << 20),
        name="short_conv",
    )(proj3, proj3, proj3, proj3, proj3, proj3, proj3, conv_w)


def _gated_proj_kernel(oa_ref, ob_ref, oc_ref, g0_ref, g1_ref, g2_ref, wa_ref, wb_ref, wc_ref, o_ref):
    f32 = jnp.float32
    acc = g0_ref[...].astype(f32) * jnp.dot(oa_ref[...], wa_ref[...], preferred_element_type=f32)
    acc += g1_ref[...].astype(f32) * jnp.dot(ob_ref[...], wb_ref[...], preferred_element_type=f32)
    acc += g2_ref[...].astype(f32) * jnp.dot(oc_ref[...], wc_ref[...], preferred_element_type=f32)
    o_ref[...] = acc.astype(o_ref.dtype)


def _gated_proj(oa, ob, oc, gates, wa, wb, wc):
    n, d = oa.shape[0], wa.shape[1]
    tm, tn = _tile(n, 512), _tile(d, 1024)
    nj = d // tn
    act = lambda a: pl.BlockSpec((tm, a.shape[1]), lambda j, i: (i, 0))
    gate = lambda br: pl.BlockSpec((tm, tn), lambda j, i: (i, br * nj + j))
    wsp = lambda w: pl.BlockSpec((w.shape[0], tn), lambda j, i: (0, j))
    ka, kb, kc = oa.shape[1], ob.shape[1], oc.shape[1]
    vmem = 2 * 2 * (tm * (ka + kb + kc) + 3 * tm * tn + (ka + kb + kc) * tn + tm * tn) + 4 * tm * tn * 4
    return pl.pallas_call(
        _gated_proj_kernel,
        out_shape=jax.ShapeDtypeStruct((n, d), jnp.bfloat16),
        grid=(nj, n // tm),
        in_specs=[act(oa), act(ob), act(oc), gate(0), gate(1), gate(2), wsp(wa), wsp(wb), wsp(wc)],
        out_specs=pl.BlockSpec((tm, tn), lambda j, i: (i, j)),
        compiler_params=_cparams(2, vmem + (4 << 20)),
        name="gated_proj",
    )(oa, ob, oc, gates, gates, gates, wa, wb, wc)


def _norm_route_kernel(x_ref, g_ref, wr_ref, h_ref, eid_ref, ew_ref):
    x = x_ref[...]
    ms = jnp.mean(x * x, axis=-1, keepdims=True)
    h = x * lax.rsqrt(ms + RMS_EPS) * g_ref[...]
    h_ref[...] = h
    logits = jnp.dot(h.astype(jnp.bfloat16), wr_ref[...], preferred_element_type=jnp.float32)
    ng, ne = N_EXPERT_GROUPS, EXPERTS_PER_GROUP
    lane = lax.broadcasted_iota(jnp.int32, logits.shape, 1)
    big = jnp.int32(1 << 20)
    is_g = lane < ng
    gl = jnp.where(is_g, logits, NEG_BIG)
    gmax = jnp.max(gl, axis=-1, keepdims=True)
    garg = jnp.min(jnp.where(is_g & (gl == gmax), lane, big), axis=-1, keepdims=True)
    g_w = 1.0 / jnp.sum(jnp.where(is_g, jnp.exp(gl - gmax), 0.0), axis=-1, keepdims=True)
    lo = ng + ne * garg
    sel = (lane >= lo) & (lane < lo + ne)
    el = jnp.where(sel, logits, NEG_BIG)
    m1 = jnp.max(el, axis=-1, keepdims=True)
    a1 = jnp.min(jnp.where(sel & (el == m1), lane, big), axis=-1, keepdims=True)
    el2 = jnp.where(lane == a1, NEG_BIG, el)
    m2 = jnp.max(el2, axis=-1, keepdims=True)
    a2 = jnp.min(jnp.where(sel & (lane != a1) & (el2 == m2), lane, big), axis=-1, keepdims=True)
    t = jnp.exp(m2 - m1)
    w1 = g_w / (1.0 + t)
    w2 = g_w * t / (1.0 + t)
    eid_ref[...] = jnp.where(lane == 0, a1 - ng, jnp.where(lane == 1, a2 - ng, 0))
    ew_ref[...] = jnp.where(lane == 0, w1, jnp.where(lane == 1, w2, 0.0))


def _norm_route(x, gain, w_route):
    n, d = x.shape
    tm = _tile(n, 256)
    return pl.pallas_call(
        _norm_route_kernel,
        out_shape=(jax.ShapeDtypeStruct((n, d), jnp.float32),
                   jax.ShapeDtypeStruct((n, LANES), jnp.int32),
                   jax.ShapeDtypeStruct((n, LANES), jnp.float32)),
        grid=(n // tm,),
        in_specs=[pl.BlockSpec((tm, d), lambda i: (i, 0)),
                  pl.BlockSpec((1, d), lambda i: (0, 0)),
                  pl.BlockSpec((d, LANES), lambda i: (0, 0))],
        out_specs=(pl.BlockSpec((tm, d), lambda i: (i, 0)),
                   pl.BlockSpec((tm, LANES), lambda i: (i, 0)),
                   pl.BlockSpec((tm, LANES), lambda i: (i, 0))),
        compiler_params=_cparams(1, 6 * tm * d * 4 + (8 << 20)),
        name="norm_route",
    )(x, gain.reshape(1, d), w_route)


def _row_gather_kernel(idx_hbm, src_hbm, o_ref, idx_smem, sem_idx, sem_rows, *, rows):
    b = pl.program_id(0)
    cp = pltpu.make_async_copy(idx_hbm.at[b], idx_smem, sem_idx)
    cp.start()
    cp.wait()

    def row_copy(r):
        return pltpu.make_async_copy(src_hbm.at[pl.ds(idx_smem[0, r], 1), :], o_ref.at[pl.ds(r, 1), :], sem_rows)

    def start(r, carry):
        row_copy(r).start()
        return carry

    def wait(r, carry):
        row_copy(r).wait()
        return carry

    lax.fori_loop(0, rows, start, 0)
    lax.fori_loop(0, rows, wait, 0)


def _row_gather(src, idx, rows):
    n_out = idx.shape[0]
    d = src.shape[1]
    nb = n_out // rows
    return pl.pallas_call(
        functools.partial(_row_gather_kernel, rows=rows),
        out_shape=jax.ShapeDtypeStruct((n_out, d), src.dtype),
        grid=(nb,),
        in_specs=[pl.BlockSpec(memory_space=pl.ANY), pl.BlockSpec(memory_space=pl.ANY)],
        out_specs=pl.BlockSpec((rows, d), lambda b: (b, 0)),
        scratch_shapes=[pltpu.SMEM((1, rows), jnp.int32),
                        pltpu.SemaphoreType.DMA, pltpu.SemaphoreType.DMA],
        compiler_params=_cparams(1, 4 * rows * d * 4 + (4 << 20)),
        name="row_gather",
    )(idx.reshape(nb, 1, rows), src)


def _moe_up_kernel(be_ref, nb_ref, x_ref, w_ref, h_ref, *, f):
    b = pl.program_id(0)

    @pl.when(b < nb_ref[0])
    def _():
        gu = jnp.dot(x_ref[...].astype(jnp.bfloat16), w_ref[...], preferred_element_type=jnp.float32)
        g, u = gu[:, :f], gu[:, f:]
        h_ref[...] = (g * (1.0 / (1.0 + jnp.exp(-g))) * u).astype(h_ref.dtype)

    @pl.when(b >= nb_ref[0])
    def _():
        h_ref[...] = jnp.zeros(h_ref.shape, h_ref.dtype)


def _moe_down_kernel(be_ref, nb_ref, h_ref, sw_ref, w_ref, y_ref):
    b = pl.program_id(0)

    @pl.when(b < nb_ref[0])
    def _():
        y = jnp.dot(h_ref[...], w_ref[...], preferred_element_type=jnp.float32)
        y_ref[...] = y * sw_ref[...]

    @pl.when(b >= nb_ref[0])
    def _():
        y_ref[...] = jnp.zeros(y_ref.shape, y_ref.dtype)


def _moe_experts(xs, slot_w, block_exp, n_blocks_used, w_gate_up, w_down):
    n_slots, d = xs.shape
    f = w_down.shape[1]
    nb = n_slots // MOE_BLOCK
    hmid = pl.pallas_call(
        functools.partial(_moe_up_kernel, f=f),
        out_shape=jax.ShapeDtypeStruct((n_slots, f), jnp.bfloat16),
        grid_spec=pltpu.PrefetchScalarGridSpec(
            num_scalar_prefetch=2, grid=(nb,),
            in_specs=[pl.BlockSpec((MOE_BLOCK, d), lambda b, be, nu: (b, 0)),
                      pl.BlockSpec((None, d, 2 * f), lambda b, be, nu: (be[b], 0, 0))],
            out_specs=pl.BlockSpec((MOE_BLOCK, f), lambda b, be, nu: (b, 0))),
        compiler_params=_cparams(1, 2 * (MOE_BLOCK * d * 4 + d * 2 * f * 2) + 3 * MOE_BLOCK * 2 * f * 4 + (4 << 20)),
        name="moe_up",
    )(block_exp, n_blocks_used, xs, w_gate_up)
    return pl.pallas_call(
        _moe_down_kernel,
        out_shape=jax.ShapeDtypeStruct((n_slots, d), jnp.float32),
        grid_spec=pltpu.PrefetchScalarGridSpec(
            num_scalar_prefetch=2, grid=(nb,),
            in_specs=[pl.BlockSpec((MOE_BLOCK, f), lambda b, be, nu: (b, 0)),
                      pl.BlockSpec((MOE_BLOCK, 1), lambda b, be, nu: (b, 0)),
                      pl.BlockSpec((None, f, d), lambda b, be, nu: (be[b], 0, 0))],
            out_specs=pl.BlockSpec((MOE_BLOCK, d), lambda b, be, nu: (b, 0))),
        compiler_params=_cparams(1, 2 * (f * d * 2 + MOE_BLOCK * d * 4) + 2 * MOE_BLOCK * d * 4 + (4 << 20)),
        name="moe_down",
    )(block_exp, n_blocks_used, hmid, slot_w.reshape(n_slots, 1), w_down)


def _add3_kernel(x_ref, a_ref, b_ref, o_ref):
    o_ref[...] = x_ref[...] + (a_ref[...] + b_ref[...])


def _add3(x, y2):
    n, d = x.shape
    tm = _tile(n, 256)
    nb = n // tm
    return pl.pallas_call(
        _add3_kernel,
        out_shape=jax.ShapeDtypeStruct((n, d), x.dtype),
        grid=(nb,),
        in_specs=[pl.BlockSpec((tm, d), lambda i: (i, 0)),
                  pl.BlockSpec((tm, d), lambda i: (i, 0)),
                  pl.BlockSpec((tm, d), lambda i: (nb + i, 0))],
        out_specs=pl.BlockSpec((tm, d), lambda i: (i, 0)),
        compiler_params=_cparams(1, 8 * tm * d * 4 + (4 << 20)),
        name="moe_combine",
    )(x, y2, y2)


def _moe(x, gain, w_route, w_gate_up, w_down):
    n, d = x.shape
    n_exp = N_EXPERT_GROUPS * EXPERTS_PER_GROUP
    h, eid, ew = _norm_route(x, gain, w_route)
    flat_e = eid[:, :2].reshape(-1)
    flat_w = ew[:, :2].reshape(-1)
    nk = 2 * n
    onehot = (flat_e[:, None] == jnp.arange(n_exp, dtype=jnp.int32)[None, :]).astype(jnp.int32)
    csum = jnp.cumsum(onehot, axis=0)
    rank = jnp.take_along_axis(csum, flat_e[:, None], axis=1)[:, 0] - 1
    counts = csum[-1]
    pcounts = (counts + MOE_BLOCK - 1) // MOE_BLOCK * MOE_BLOCK
    pend = jnp.cumsum(pcounts)
    pstart = pend - pcounts
    dest = pstart[flat_e] + rank
    n_blocks = -(-nk // MOE_BLOCK) + n_exp
    n_slots = n_blocks * MOE_BLOCK
    flat_tok = jnp.arange(nk, dtype=jnp.int32) // 2
    slot_tok = jnp.zeros((n_slots,), jnp.int32).at[dest].set(flat_tok)
    slot_w = jnp.zeros((n_slots,), jnp.float32).at[dest].set(flat_w)
    block_exp = jnp.minimum(jnp.searchsorted(pend, jnp.arange(n_blocks, dtype=jnp.int32) * MOE_BLOCK, side='right'),
                            n_exp - 1).astype(jnp.int32)
    n_used = (pend[-1] // MOE_BLOCK).astype(jnp.int32).reshape(1)
    xs = _row_gather(h, slot_tok, MOE_BLOCK)
    ys = _moe_experts(xs, slot_w, block_exp, n_used, w_gate_up, w_down)
    back = dest.reshape(n, 2).T.reshape(-1)
    y2 = _row_gather(ys, back, _tile(n, MOE_BLOCK))
    return _add3(x, y2)


def _alibi_slopes(n):
    return [2.0 ** (-8.0 * (i + 1) / n) for i in range(n)]


def _token_mixers(x, l, p, n_seq):
    n, d = x.shape
    s_len = n // n_seq
    bf16 = jnp.bfloat16
    n_ga = len(DIL_GROUPS)
    a_w = A_HEADS * HEAD_DIM
    a_cols = n_ga * a_w
    b_qk = B_HEADS * 2 * B_QK_DIM
    b_w = B_HEADS * 2 * B_QK_DIM
    in_cols = p['w_in'].shape[2]
    slopes = _alibi_slopes(n_ga * A_HEADS + B_HEADS)

    h = _rmsnorm(x, p['norm_mix'][l])
    proj = _matmul(h, p['w_in'][l].astype(bf16), bf16, name="in_proj")
    gates = _matmul(h, p['w_gate'][l].astype(bf16), bf16, sigmoid=True, name="gate_proj")
    proj3 = proj.reshape(n_seq, s_len, in_cols)

    outs, lses = [], []
    for g, (_, dilation) in enumerate(DIL_GROUPS):
        o, lse = _attn_a_group(proj3, p['qnorm_a'][l], p['knorm_a'][l], group=g, dilation=dilation,
                               slopes=tuple(slopes[g * A_HEADS:(g + 1) * A_HEADS]), in_cols=in_cols,
                               n_heads=A_HEADS)
        outs.append(o.reshape(n, a_w))
        lses.append(lse.reshape(n, LANES))
    oa = _combine_a(outs, lses, A_HEADS)

    lam_init = 0.8 - 0.6 * math.exp(-0.3 * l)
    lam_vecs = jnp.stack([p['lambda_q1'][l], p['lambda_k1'][l], p['lambda_q2'][l], p['lambda_k2'][l]])
    ob = _attn_b(proj3, p['qnorm_b'][l], p['knorm_b'][l], lam_vecs, p['subln_b'][l],
                 slopes[n_ga * A_HEADS:], lam_init=lam_init,
                 q_col=3 * a_cols, k_col=3 * a_cols + b_qk, v_col=3 * a_cols + 2 * b_qk,
                 n_heads=B_HEADS).reshape(n, b_w)

    c0 = 3 * a_cols + 2 * b_qk + b_w
    oc = _conv(proj3, p['conv_w'][l], u_col=c0, b_col=c0 + C_WIDTH, c_col=c0 + 2 * C_WIDTH).reshape(n, C_WIDTH)

    merged = _gated_proj(oa, ob, oc, gates, p['w_proj_a'][l].astype(bf16), p['w_proj_b'][l].astype(bf16),
                         p['w_proj_c'][l].astype(bf16))
    return _matmul(merged, p['w_out'][l].astype(bf16), jnp.float32, residual=x, tn=512, name="out_proj")


def _route_weights(w_group, w_expert):
    d = w_group.shape[0]
    used = w_group.shape[1] + w_expert.shape[1]
    return jnp.concatenate([w_group, w_expert, jnp.zeros((d, LANES - used), w_group.dtype)],
                           axis=1).astype(jnp.bfloat16)


def kernel(x_prompt, x_sample, norm_mix, w_in, qnorm_a, knorm_a, qnorm_b, knorm_b, lambda_q1, lambda_k1,
           lambda_q2, lambda_k2, subln_b, conv_w, w_proj_a, w_proj_b, w_proj_c, w_gate, w_out, norm_ffn,
           w_route_group, w_route_expert, w_gate_up, w_down):
    p = dict(norm_mix=norm_mix, w_in=w_in, qnorm_a=qnorm_a, knorm_a=knorm_a, qnorm_b=qnorm_b, knorm_b=knorm_b,
             lambda_q1=lambda_q1, lambda_k1=lambda_k1, lambda_q2=lambda_q2, lambda_k2=lambda_k2,
             subln_b=subln_b, conv_w=conv_w, w_proj_a=w_proj_a, w_proj_b=w_proj_b, w_proj_c=w_proj_c,
             w_gate=w_gate, w_out=w_out)
    bp, s_len, d = x_prompt.shape
    bs = x_sample.shape[0]
    assert x_sample.shape[1:] == (s_len, d)
    n_seq = bp + bs
    x = jnp.concatenate([x_prompt.reshape(bp * s_len, d), x_sample.reshape(bs * s_len, d)], axis=0)
    for l in range(norm_mix.shape[0]):
        x = _token_mixers(x, l, p, n_seq)
        x = _moe(x, norm_ffn[l], _route_weights(w_route_group[l], w_route_expert[l]),
                 w_gate_up[l].astype(jnp.bfloat16), w_down[l].astype(jnp.bfloat16))
    return (x[:bp * s_len].reshape(bp, s_len, d), x[bp * s_len:].reshape(bs, s_len, d))
```

```python
import functools
import math

import jax
import jax.numpy as jnp
from jax import lax
from jax.experimental import pallas as pl
from jax.experimental.pallas import tpu as pltpu

HEAD_DIM = 128
DIL_GROUPS = ((128, 1), (512, 4), (2048, 16))
A_HEADS = 8
B_HEADS = 8
B_QK_DIM = 64
C_WIDTH = 2048
N_EXPERT_GROUPS = 4
EXPERTS_PER_GROUP = 8
D_FF_EXPERT = 1024
MOE_BLOCK = 256
RMS_EPS = 1e-6
NEG_BIG = -1e30

LANES = 128
V7X_VMEM_LIMIT_BYTES = 56 * 1024 * 1024

A_HALF = DIL_GROUPS[0][0] // (2 * DIL_GROUPS[0][1])


def _cparams(n_grid_dims, vmem_bytes):
    return pltpu.CompilerParams(
        dimension_semantics=("arbitrary",) * n_grid_dims,
        vmem_limit_bytes=int(min(max(vmem_bytes, 16 * 1024 * 1024), V7X_VMEM_LIMIT_BYTES)),
    )


def _tile(n, pref):
    t = min(n, pref)
    while n % t:
        t //= 2
    return t


def _rmsnorm_kernel(x_ref, g_ref, o_ref):
    x = x_ref[...]
    ms = jnp.mean(x * x, axis=-1, keepdims=True)
    o_ref[...] = (x * lax.rsqrt(ms + RMS_EPS) * g_ref[...]).astype(o_ref.dtype)


def _rmsnorm(x, gain):
    n, d = x.shape
    tm = _tile(n, 256)
    return pl.pallas_call(
        _rmsnorm_kernel,
        out_shape=jax.ShapeDtypeStruct((n, d), jnp.bfloat16),
        grid=(n // tm,),
        in_specs=[pl.BlockSpec((tm, d), lambda i: (i, 0)),
                  pl.BlockSpec((1, d), lambda i: (0, 0))],
        out_specs=pl.BlockSpec((tm, d), lambda i: (i, 0)),
        compiler_params=_cparams(1, 4 * tm * d * 4),
        name="rmsnorm",
    )(x, gain.reshape(1, d))


def _matmul_kernel(*refs, sigmoid, has_res):
    if has_res:
        x_ref, w_ref, r_ref, o_ref = refs
    else:
        x_ref, w_ref, o_ref = refs
    acc = jnp.dot(x_ref[...], w_ref[...], preferred_element_type=jnp.float32)
    if sigmoid:
        acc = 1.0 / (1.0 + jnp.exp(-acc))
    if has_res:
        acc = acc + r_ref[...]
    o_ref[...] = acc.astype(o_ref.dtype)


def _matmul(x, w, layer, out_dtype, *, sigmoid=False, residual=None, tm=1024, tn=1024, name="matmul"):
    n, k = x.shape
    c = w.shape[2]
    tm, tn = _tile(n, tm), _tile(c, tn)
    in_specs = [pl.BlockSpec((tm, k), lambda j, i: (i, 0)),
                pl.BlockSpec((None, k, tn), lambda j, i: (layer, 0, j))]
    args = [x, w]
    out_bytes = jnp.dtype(out_dtype).itemsize
    vmem = 2 * (tm * k * 2 + k * tn * 2 + tm * tn * out_bytes) + 2 * tm * tn * 4
    if residual is not None:
        in_specs.append(pl.BlockSpec((tm, tn), lambda j, i: (i, j)))
        args.append(residual)
        vmem += 2 * tm * tn * 4
    return pl.pallas_call(
        functools.partial(_matmul_kernel, sigmoid=sigmoid, has_res=residual is not None),
        out_shape=jax.ShapeDtypeStruct((n, c), out_dtype),
        grid=(c // tn, n // tm),
        in_specs=in_specs,
        out_specs=pl.BlockSpec((tm, tn), lambda j, i: (i, j)),
        compiler_params=_cparams(2, vmem + (4 << 20)),
        name=name,
    )(*args)


def _head_rmsnorm(x, gain):
    ms = jnp.mean(x * x, axis=-1, keepdims=True)
    return x * lax.rsqrt(ms + RMS_EPS) * gain


def _attn_a_kernel(q_ref, kp_ref, km_ref, kn_ref, vp_ref, vm_ref, vn_ref, gq_ref, gk_ref,
                   o_ref, lse_ref, qn_scr, kn_scr, v_scr, *, dilation, slopes, tu, sub):
    half = A_HALF
    i = pl.program_id(2)
    n_i = pl.num_programs(2)
    n_heads = len(slopes)
    scale = HEAD_DIM ** -0.5

    for h in range(n_heads):
        cs = slice(h * HEAD_DIM, (h + 1) * HEAD_DIM)
        qn_scr[:, cs] = _head_rmsnorm(q_ref[:, cs].astype(jnp.float32), gq_ref[...]).astype(qn_scr.dtype)
        for off, ref, rows in ((0, kp_ref, half), (half, km_ref, tu), (half + tu, kn_ref, half)):
            kn_scr[off:off + rows, cs] = _head_rmsnorm(ref[:, cs].astype(jnp.float32),
                                                       gk_ref[...]).astype(kn_scr.dtype)
    v_scr[0:half, :] = vp_ref[...]
    v_scr[half:half + tu, :] = vm_ref[...]
    v_scr[half + tu:, :] = vn_ref[...]

    kw = sub + 2 * half
    ii = lax.broadcasted_iota(jnp.int32, (sub, kw), 0)
    jj = lax.broadcasted_iota(jnp.int32, (sub, kw), 1)
    rel = jnp.abs(jj - half - ii)
    band = rel <= half
    dist = (rel * dilation).astype(jnp.float32)
    lane = lax.broadcasted_iota(jnp.int32, (sub, LANES), 1)

    def body(t, carry):
        a = pl.multiple_of(t * sub, sub)
        kpos = a + jj
        valid = band & ((kpos >= half) | (i > 0)) & ((kpos < tu + half) | (i < n_i - 1))
        lse_tile = jnp.zeros((sub, LANES), jnp.float32)
        for h in range(n_heads):
            cs = slice(h * HEAD_DIM, (h + 1) * HEAD_DIM)
            qh = qn_scr[pl.ds(a, sub), cs]
            kh = kn_scr[pl.ds(a, kw), cs]
            vh = v_scr[pl.ds(a, kw), cs]
            s = lax.dot_general(qh, kh, (((1,), (1,)), ((), ())), preferred_element_type=jnp.float32)
            s = jnp.where(valid, s * scale - slopes[h] * dist, NEG_BIG)
            m = jnp.max(s, axis=-1, keepdims=True)
            p = jnp.exp(s - m)
            l = jnp.sum(p, axis=-1, keepdims=True)
            o = jnp.dot(p.astype(vh.dtype), vh, preferred_element_type=jnp.float32) / l
            o_ref[pl.ds(a, sub), cs] = o.astype(o_ref.dtype)
            lse_tile = jnp.where(lane == h, m + jnp.log(l), lse_tile)
        lse_ref[pl.ds(a, sub), :] = lse_tile
        return carry

    lax.fori_loop(0, tu // sub, body, 0)


def _attn_a_group(proj3, gq, gk, *, group, dilation, slopes, in_cols, n_heads):
    n_seq, s_len, _ = proj3.shape
    half = A_HALF
    u_len = s_len // dilation
    width = n_heads * HEAD_DIM
    a_cols = len(DIL_GROUPS) * width
    tu = _tile(u_len, 512)
    sub = min(tu, 128)
    q_blk = group
    k_blk = a_cols // width + group
    v_blk = 2 * a_cols // width + group
    if dilation > 1:
        proj3 = jnp.concatenate([proj3[:, :, c * width:(c + 1) * width] for c in (q_blk, k_blk, v_blk)], axis=-1)
        in_cols, q_blk, k_blk, v_blk = 3 * width, 0, 1, 2
    pv = proj3.reshape(n_seq, u_len, dilation * in_cols)
    blocks_per_row = in_cols // width
    nh = tu // half
    n_halo = u_len // half

    def main(col):
        return pl.BlockSpec((None, tu, width), lambda b, r, i: (b, i, r * blocks_per_row + col))

    def prev(col):
        return pl.BlockSpec((None, half, width),
                            lambda b, r, i: (b, jnp.maximum(i * nh - 1, 0), r * blocks_per_row + col))

    def nxt(col):
        return pl.BlockSpec((None, half, width),
                            lambda b, r, i: (b, jnp.minimum((i + 1) * nh, n_halo - 1), r * blocks_per_row + col))

    gspec = pl.BlockSpec((1, HEAD_DIM), lambda b, r, i: (0, 0))
    o, lse = pl.pallas_call(
        functools.partial(_attn_a_kernel, dilation=dilation, slopes=slopes, tu=tu, sub=sub),
        out_shape=(jax.ShapeDtypeStruct((n_seq, u_len, dilation * width), jnp.bfloat16),
                   jax.ShapeDtypeStruct((n_seq, u_len, dilation * LANES), jnp.float32)),
        grid=(n_seq, dilation, u_len // tu),
        in_specs=[main(q_blk), prev(k_blk), main(k_blk), nxt(k_blk),
                  prev(v_blk), main(v_blk), nxt(v_blk), gspec, gspec],
        out_specs=(pl.BlockSpec((None, tu, width), lambda b, r, i: (b, i, r)),
                   pl.BlockSpec((None, tu, LANES), lambda b, r, i: (b, i, r))),
        scratch_shapes=[pltpu.VMEM((tu, width), jnp.bfloat16),
                        pltpu.VMEM((tu + 2 * half, width), jnp.bfloat16),
                        pltpu.VMEM((tu + 2 * half, width), jnp.bfloat16)],
        compiler_params=_cparams(3, 32 << 20),
        name=f"attn_a_g{group}",
    )(pv, pv, pv, pv, pv, pv, pv, gq.reshape(1, HEAD_DIM), gk.reshape(1, HEAD_DIM))
    return o.reshape(n_seq, s_len, width), lse.reshape(n_seq, s_len, LANES)


def _combine_a_kernel(o0_ref, o1_ref, o2_ref, l0_ref, l1_ref, l2_ref, out_ref, *, n_heads):
    l0, l1, l2 = l0_ref[...], l1_ref[...], l2_ref[...]
    m = jnp.maximum(jnp.maximum(l0, l1), l2)
    e0, e1, e2 = jnp.exp(l0 - m), jnp.exp(l1 - m), jnp.exp(l2 - m)
    den = e0 + e1 + e2
    w0, w1, w2 = e0 / den, e1 / den, e2 / den
    for h in range(n_heads):
        cs = slice(h * HEAD_DIM, (h + 1) * HEAD_DIM)
        acc = (w0[:, h:h + 1] * o0_ref[:, cs].astype(jnp.float32)
               + w1[:, h:h + 1] * o1_ref[:, cs].astype(jnp.float32)
               + w2[:, h:h + 1] * o2_ref[:, cs].astype(jnp.float32))
        out_ref[:, cs] = acc.astype(out_ref.dtype)


def _combine_a(outs, lses, n_heads):
    n, width = outs[0].shape
    tm = _tile(n, 512)
    ospec = pl.BlockSpec((tm, width), lambda i: (i, 0))
    lspec = pl.BlockSpec((tm, LANES), lambda i: (i, 0))
    return pl.pallas_call(
        functools.partial(_combine_a_kernel, n_heads=n_heads),
        out_shape=jax.ShapeDtypeStruct((n, width), jnp.bfloat16),
        grid=(n // tm,),
        in_specs=[ospec, ospec, ospec, lspec, lspec, lspec],
        out_specs=ospec,
        compiler_params=_cparams(1, 24 << 20),
        name="combine_a",
    )(*outs, *lses)


_B_FEAT = 4


def _split_hi_lo(x):
    hi = x.astype(jnp.bfloat16)
    lo = (x - hi.astype(jnp.float32)).astype(jnp.bfloat16)
    return hi.astype(jnp.float32), lo.astype(jnp.float32)


_LOG2E = 1.4426950408889634


def _attn_b_kernel(q_ref, k_ref, v_ref, gq_ref, gk_ref, lam_ref, gs_ref, slope_ref, o_ref,
                   k_scr, vt_scr, q_scr, dist_scr, m_scr, l_scr, acc_scr, *, lam_init, tq, tk, s_len):
    i = pl.program_id(2)
    dq = B_QK_DIM
    n_kt = s_len // tk
    slope = slope_ref[...] * _LOG2E
    slope1 = slope[:, 0:1]

    lane_q = lax.broadcasted_iota(jnp.int32, (tq, 2 * dq), 1)
    lane_k = lax.broadcasted_iota(jnp.int32, (tk, 2 * dq), 1)

    def norm_maps(x, gain, lane):
        in0 = lane < dq
        sq = x * x
        ms0 = jnp.sum(jnp.where(in0, sq, 0.0), axis=-1, keepdims=True) / dq
        ms1 = jnp.sum(jnp.where(in0, 0.0, sq), axis=-1, keepdims=True) / dq
        inv = jnp.where(in0, lax.rsqrt(ms0 + RMS_EPS), lax.rsqrt(ms1 + RMS_EPS))
        return x * inv * gain

    @pl.when(i == 0)
    def _():
        def kbody(t, carry):
            r0 = pl.multiple_of(t * tk, tk)
            kn = norm_maps(k_ref[pl.ds(r0, tk), :].astype(jnp.float32), gk_ref[...], lane_k)
            b = lax.broadcasted_iota(jnp.int32, (tk, 2 * dq), 0).astype(jnp.float32)
            l_hi, l_lo = _split_hi_lo(slope * b)
            r_hi, r_lo = _split_hi_lo(slope * (tk - 1 - b))
            for m in range(2):
                f0 = (1 - m) * dq
                feat = jnp.where(lane_k == f0, l_hi,
                       jnp.where(lane_k == f0 + 1, l_lo,
                       jnp.where(lane_k == f0 + 2, r_hi,
                       jnp.where(lane_k == f0 + 3, r_lo, 0.0))))
                own = (lane_k >= m * dq) & (lane_k < (m + 1) * dq)
                k_scr[m, pl.ds(r0, tk), :] = jnp.where(own, kn, feat).astype(k_scr.dtype)
            vt_scr[t] = v_ref[pl.ds(r0, tk), :].astype(jnp.float32).T.astype(vt_scr.dtype)
            return carry
        lax.fori_loop(0, n_kt, kbody, 0)
        a = lax.broadcasted_iota(jnp.int32, (tq, tk), 0)
        b = lax.broadcasted_iota(jnp.int32, (tq, tk), 1)
        dist_scr[...] = jnp.abs(a - b).astype(jnp.float32)

    qn = norm_maps(q_ref[...].astype(jnp.float32), gq_ref[...], lane_q) * (dq ** -0.5 * _LOG2E)
    for m in range(2):
        f0 = (1 - m) * dq
        own = (lane_q >= m * dq) & (lane_q < (m + 1) * dq)
        left = (lane_q == f0) | (lane_q == f0 + 1)
        right = (lane_q == f0 + 2) | (lane_q == f0 + 3)
        q_scr[m, 0] = jnp.where(own, qn, 0.0).astype(q_scr.dtype)
        q_scr[m, 1] = jnp.where(own, qn, jnp.where(left, 1.0, 0.0)).astype(q_scr.dtype)
        q_scr[m, 2] = jnp.where(own, qn, jnp.where(right, 1.0, 0.0)).astype(q_scr.dtype)

    m_scr[...] = jnp.full(m_scr.shape, NEG_BIG, jnp.float32)
    l_scr[...] = jnp.zeros(l_scr.shape, jnp.float32)
    acc_scr[...] = jnp.zeros(acc_scr.shape, jnp.float32)

    q_pos = i * tq + lax.broadcasted_iota(jnp.int32, (1, tq), 1)

    def tile(j, ver):
        r0 = pl.multiple_of(j * tk, tk)
        vt = vt_scr[j]
        j0 = j * tk
        for m in range(2):
            kt = k_scr[m, pl.ds(r0, tk), :]
            s = lax.dot_general(kt, q_scr[m, ver], (((1,), (1,)), ((), ())),
                                preferred_element_type=jnp.float32)
            if ver == 0:
                s = s - slope1 * dist_scr[...]
                shift = jnp.zeros((1, tq), jnp.float32)
            elif ver == 1:
                shift = slope1 * (j0 - q_pos).astype(jnp.float32)
            else:
                shift = slope1 * (q_pos - j0 - (tk - 1)).astype(jnp.float32)
            m_old = m_scr[m]
            m_new = jnp.maximum(m_old, jnp.max(s, axis=0, keepdims=True) + shift)
            p = jnp.exp2(s - (m_new - shift))
            alpha = jnp.exp2(m_old - m_new)
            l_scr[m] = alpha * l_scr[m] + jnp.sum(p, axis=0, keepdims=True)
            acc_scr[m] = alpha * acc_scr[m] + jnp.dot(vt, p.astype(vt.dtype),
                                                      preferred_element_type=jnp.float32)
            m_scr[m] = m_new

    i_kt = i * (tq // tk)
    tile(i_kt, 0)

    def left_body(j, carry):
        tile(j, 1)
        return carry

    def right_body(j, carry):
        tile(j, 2)
        return carry

    lax.fori_loop(0, i_kt, left_body, 0)
    lax.fori_loop(i_kt + 1, n_kt, right_body, 0)

    lam_v = lam_ref[...]
    lam = (jnp.exp(jnp.sum(lam_v[0:1] * lam_v[1:2], axis=-1, keepdims=True))
           - jnp.exp(jnp.sum(lam_v[2:3] * lam_v[3:4], axis=-1, keepdims=True)) + lam_init)
    out = acc_scr[0] * (1.0 / l_scr[0]) - lam * (acc_scr[1] * (1.0 / l_scr[1]))
    ms = jnp.mean(out * out, axis=0, keepdims=True)
    out = out * lax.rsqrt(ms + RMS_EPS) * gs_ref[...] * (1.0 - lam_init)
    o_ref[...] = out.T.astype(o_ref.dtype)


def _attn_b(proj3, gq, gk, lam_vecs, subln, slopes, *, lam_init, q_col, k_col, v_col, n_heads):
    n_seq, s_len, _ = proj3.shape
    hw = 2 * B_QK_DIM
    tq = tk = _tile(s_len, 1024)
    qb, kb, vb = q_col // hw, k_col // hw, v_col // hw
    gq2 = jnp.tile(gq, 2).reshape(1, hw)
    gk2 = jnp.tile(gk, 2).reshape(1, hw)
    slope_arr = jnp.broadcast_to(jnp.asarray(slopes, jnp.float32)[:, None, None], (n_heads, 1, LANES))
    small = lambda shape: pl.BlockSpec(shape, lambda b, h, i: (0, 0))
    return pl.pallas_call(
        functools.partial(_attn_b_kernel, lam_init=lam_init, tq=tq, tk=tk, s_len=s_len),
        out_shape=jax.ShapeDtypeStruct((n_seq, s_len, n_heads * hw), jnp.bfloat16),
        grid=(n_seq, n_heads, s_len // tq),
        in_specs=[pl.BlockSpec((None, tq, hw), lambda b, h, i: (b, i, qb + h)),
                  pl.BlockSpec((None, s_len, hw), lambda b, h, i: (b, 0, kb + h)),
                  pl.BlockSpec((None, s_len, hw), lambda b, h, i: (b, 0, vb + h)),
                  small((1, hw)), small((1, hw)), small((4, B_QK_DIM)), small((hw, 1)),
                  pl.BlockSpec((None, 1, LANES), lambda b, h, i: (h, 0, 0))],
        out_specs=pl.BlockSpec((None, tq, hw), lambda b, h, i: (b, i, h)),
        scratch_shapes=[pltpu.VMEM((2, s_len, hw), jnp.bfloat16),
                        pltpu.VMEM((s_len // tk, hw, tk), jnp.bfloat16),
                        pltpu.VMEM((2, 3, tq, hw), jnp.bfloat16),
                        pltpu.VMEM((tk, tq), jnp.float32),
                        pltpu.VMEM((2, 1, tq), jnp.float32),
                        pltpu.VMEM((2, 1, tq), jnp.float32),
                        pltpu.VMEM((2, hw, tq), jnp.float32)],
        compiler_params=_cparams(3, V7X_VMEM_LIMIT_BYTES),
        name="attn_b",
    )(proj3, proj3, proj3, gq2, gk2, lam_vecs, subln.reshape(hw, 1), slope_arr)


_CONV_HALO = 16


def _conv_kernel(u_ref, b_ref, c_ref, up_ref, cp_ref, un_ref, cn_ref, w_ref, o_ref, *, tm):
    i = pl.program_id(1)
    n_i = pl.num_programs(1)
    f32 = jnp.float32
    v = c_ref[...].astype(f32) * u_ref[...].astype(f32)
    v_prev = (cp_ref[_CONV_HALO - 1:_CONV_HALO, :].astype(f32) * up_ref[_CONV_HALO - 1:_CONV_HALO, :].astype(f32))
    v_next = cn_ref[0:1, :].astype(f32) * un_ref[0:1, :].astype(f32)
    v_prev = jnp.where(i > 0, v_prev, 0.0)
    v_next = jnp.where(i < n_i - 1, v_next, 0.0)
    row = lax.broadcasted_iota(jnp.int32, v.shape, 0)
    down = jnp.where(row == 0, v_prev, pltpu.roll(v, 1, axis=0))
    up = jnp.where(row == tm - 1, v_next, pltpu.roll(v, tm - 1, axis=0))
    w = w_ref[...]
    y = down * w[0:1] + v * w[1:2] + up * w[2:3]
    o_ref[...] = (b_ref[...].astype(f32) * y).astype(o_ref.dtype)


def _conv(proj3, conv_w, *, u_col, b_col, c_col):
    n_seq, s_len, _ = proj3.shape
    cw = conv_w.shape[1]
    tm = _tile(s_len, 512)
    tc = _tile(cw, 1024)
    ncb = cw // tc
    nh = tm // _CONV_HALO
    n_halo = s_len // _CONV_HALO

    def main(col):
        return pl.BlockSpec((None, tm, tc), lambda b, i, c: (b, i, col // tc + c))

    def prev(col):
        return pl.BlockSpec((None, _CONV_HALO, tc),
                            lambda b, i, c: (b, jnp.maximum(i * nh - 1, 0), col // tc + c))

    def nxt(col):
        return pl.BlockSpec((None, _CONV_HALO, tc),
                            lambda b, i, c: (b, jnp.minimum((i + 1) * nh, n_halo - 1), col // tc + c))

    return pl.pallas_call(
        functools.partial(_conv_kernel, tm=tm),
        out_shape=jax.ShapeDtypeStruct((n_seq, s_len, cw), jnp.bfloat16),
        grid=(n_seq, s_len // tm, ncb),
        in_specs=[main(u_col), main(b_col), main(c_col), prev(u_col), prev(c_col), nxt(u_col), nxt(c_col),
                  pl.BlockSpec((3, tc), lambda b, i, c: (0, c))],
        out_specs=pl.BlockSpec((None, tm, tc), lambda b, i, c: (b, i, c)),
        compiler_params=_cparams(3, 32 << 20),
        name="short_conv",
    )(proj3, proj3, proj3, proj3, proj3, proj3, proj3, conv_w)


def _gated_proj_kernel(oa_ref, ob_ref, oc_ref, g0_ref, g1_ref, g2_ref, wa_ref, wb_ref, wc_ref, o_ref):
    f32 = jnp.float32
    acc = g0_ref[...].astype(f32) * jnp.dot(oa_ref[...], wa_ref[...], preferred_element_type=f32)
    acc += g1_ref[...].astype(f32) * jnp.dot(ob_ref[...], wb_ref[...], preferred_element_type=f32)
    acc += g2_ref[...].astype(f32) * jnp.dot(oc_ref[...], wc_ref[...], preferred_element_type=f32)
    o_ref[...] = acc.astype(o_ref.dtype)


def _gated_proj(oa, ob, oc, gates, wa, wb, wc, layer):
    n, d = oa.shape[0], wa.shape[2]
    tm, tn = _tile(n, 512), _tile(d, 1024)
    nj = d // tn
    act = lambda a: pl.BlockSpec((tm, a.shape[1]), lambda j, i: (i, 0))
    gate = lambda br: pl.BlockSpec((tm, tn), lambda j, i: (i, br * nj + j))
    wsp = lambda w: pl.BlockSpec((None, w.shape[1], tn), lambda j, i: (layer, 0, j))
    ka, kb, kc = oa.shape[1], ob.shape[1], oc.shape[1]
    vmem = 2 * 2 * (tm * (ka + kb + kc) + 3 * tm * tn + (ka + kb + kc) * tn + tm * tn) + 4 * tm * tn * 4
    return pl.pallas_call(
        _gated_proj_kernel,
        out_shape=jax.ShapeDtypeStruct((n, d), jnp.bfloat16),
        grid=(nj, n // tm),
        in_specs=[act(oa), act(ob), act(oc), gate(0), gate(1), gate(2), wsp(wa), wsp(wb), wsp(wc)],
        out_specs=pl.BlockSpec((tm, tn), lambda j, i: (i, j)),
        compiler_params=_cparams(2, vmem + (4 << 20)),
        name="gated_proj",
    )(oa, ob, oc, gates, gates, gates, wa, wb, wc)


def _pack_bf16_pairs(x):
    c = x.shape[1] // 2
    bits = lax.bitcast_convert_type(x.astype(jnp.bfloat16).astype(jnp.float32), jnp.uint32)
    return (bits[:, :c] & jnp.uint32(0xFFFF0000)) | (bits[:, c:] >> 16)


def _unpack_bf16_pairs(p):
    hi = lax.bitcast_convert_type(p & jnp.uint32(0xFFFF0000), jnp.float32)
    lo = lax.bitcast_convert_type(p << 16, jnp.float32)
    return hi, lo


def _norm_route_kernel(x_ref, g_ref, wr_ref, h_ref, eid_ref, ew_ref):
    x = x_ref[...]
    ms = jnp.mean(x * x, axis=-1, keepdims=True)
    h = x * lax.rsqrt(ms + RMS_EPS) * g_ref[...]
    h_ref[...] = _pack_bf16_pairs(h)
    logits = jnp.dot(h.astype(jnp.bfloat16), wr_ref[...], preferred_element_type=jnp.float32)
    ng, ne = N_EXPERT_GROUPS, EXPERTS_PER_GROUP
    lane = lax.broadcasted_iota(jnp.int32, logits.shape, 1)
    big = jnp.int32(1 << 20)
    is_g = lane < ng
    gl = jnp.where(is_g, logits, NEG_BIG)
    gmax = jnp.max(gl, axis=-1, keepdims=True)
    garg = jnp.min(jnp.where(is_g & (gl == gmax), lane, big), axis=-1, keepdims=True)
    g_w = 1.0 / jnp.sum(jnp.where(is_g, jnp.exp(gl - gmax), 0.0), axis=-1, keepdims=True)
    lo = ng + ne * garg
    sel = (lane >= lo) & (lane < lo + ne)
    el = jnp.where(sel, logits, NEG_BIG)
    m1 = jnp.max(el, axis=-1, keepdims=True)
    a1 = jnp.min(jnp.where(sel & (el == m1), lane, big), axis=-1, keepdims=True)
    el2 = jnp.where(lane == a1, NEG_BIG, el)
    m2 = jnp.max(el2, axis=-1, keepdims=True)
    a2 = jnp.min(jnp.where(sel & (lane != a1) & (el2 == m2), lane, big), axis=-1, keepdims=True)
    t = jnp.exp(m2 - m1)
    w1 = g_w / (1.0 + t)
    w2 = g_w * t / (1.0 + t)
    eid_ref[...] = jnp.where(lane == 0, a1 - ng, jnp.where(lane == 1, a2 - ng, 0))
    ew_ref[...] = jnp.where(lane == 0, w1, jnp.where(lane == 1, w2, 0.0))


def _norm_route(x, gain, w_route):
    n, d = x.shape
    tm = _tile(n, 256)
    return pl.pallas_call(
        _norm_route_kernel,
        out_shape=(jax.ShapeDtypeStruct((n, d // 2), jnp.uint32),
                   jax.ShapeDtypeStruct((n, LANES), jnp.int32),
                   jax.ShapeDtypeStruct((n, LANES), jnp.float32)),
        grid=(n // tm,),
        in_specs=[pl.BlockSpec((tm, d), lambda i: (i, 0)),
                  pl.BlockSpec((1, d), lambda i: (0, 0)),
                  pl.BlockSpec((d, LANES), lambda i: (0, 0))],
        out_specs=(pl.BlockSpec((tm, d // 2), lambda i: (i, 0)),
                   pl.BlockSpec((tm, LANES), lambda i: (i, 0)),
                   pl.BlockSpec((tm, LANES), lambda i: (i, 0))),
        compiler_params=_cparams(1, 6 * tm * d * 4 + (8 << 20)),
        name="norm_route",
    )(x, gain.reshape(1, d), w_route)


def _row_gather_kernel(idx_hbm, src_hbm, o_ref, idx_smem, sem_idx, sem_rows, *, rows):
    b = pl.program_id(0)
    cp = pltpu.make_async_copy(idx_hbm.at[b], idx_smem, sem_idx)
    cp.start()
    cp.wait()

    def row_copy(r):
        return pltpu.make_async_copy(src_hbm.at[pl.ds(idx_smem[0, r], 1), :], o_ref.at[pl.ds(r, 1), :], sem_rows)

    def start(r, carry):
        row_copy(r).start()
        return carry

    def wait(r, carry):
        row_copy(r).wait()
        return carry

    lax.fori_loop(0, rows, start, 0)
    lax.fori_loop(0, rows, wait, 0)


def _row_gather(src, idx, rows):
    n_out = idx.shape[0]
    d = src.shape[1]
    nb = n_out // rows
    return pl.pallas_call(
        functools.partial(_row_gather_kernel, rows=rows),
        out_shape=jax.ShapeDtypeStruct((n_out, d), src.dtype),
        grid=(nb,),
        in_specs=[pl.BlockSpec(memory_space=pl.ANY), pl.BlockSpec(memory_space=pl.ANY)],
        out_specs=pl.BlockSpec((rows, d), lambda b: (b, 0)),
        scratch_shapes=[pltpu.SMEM((1, rows), jnp.int32),
                        pltpu.SemaphoreType.DMA, pltpu.SemaphoreType.DMA],
        compiler_params=_cparams(1, 4 * rows * d * 4 + (4 << 20)),
        name="row_gather",
    )(idx.reshape(nb, 1, rows), src)


def _moe_up_kernel(be_ref, nb_ref, x_ref, w_ref, h_ref, *, f):
    b = pl.program_id(0)

    @pl.when(b < nb_ref[0])
    def _():
        x_hi, x_lo = _unpack_bf16_pairs(x_ref[...])
        half = x_hi.shape[1]
        gu = (jnp.dot(x_hi.astype(jnp.bfloat16), w_ref[:half, :], preferred_element_type=jnp.float32)
              + jnp.dot(x_lo.astype(jnp.bfloat16), w_ref[half:, :], preferred_element_type=jnp.float32))
        g, u = gu[:, :f], gu[:, f:]
        h_ref[...] = (g * (1.0 / (1.0 + jnp.exp(-g))) * u).astype(h_ref.dtype)

    @pl.when(b >= nb_ref[0])
    def _():
        h_ref[...] = jnp.zeros(h_ref.shape, h_ref.dtype)


def _moe_down_kernel(be_ref, nb_ref, h_ref, sw_ref, w_ref, y_ref):
    b = pl.program_id(0)

    @pl.when(b < nb_ref[0])
    def _():
        y = jnp.dot(h_ref[...], w_ref[...], preferred_element_type=jnp.float32)
        y_ref[...] = _pack_bf16_pairs(y * sw_ref[...])

    @pl.when(b >= nb_ref[0])
    def _():
        y_ref[...] = jnp.zeros(y_ref.shape, y_ref.dtype)


def _moe_experts(xs, slot_w, block_exp, n_blocks_used, w_gate_up, w_down, layer):
    n_slots, dh = xs.shape
    d = 2 * dh
    f = w_down.shape[2]
    nb = n_slots // MOE_BLOCK
    hmid = pl.pallas_call(
        functools.partial(_moe_up_kernel, f=f),
        out_shape=jax.ShapeDtypeStruct((n_slots, f), jnp.bfloat16),
        grid_spec=pltpu.PrefetchScalarGridSpec(
            num_scalar_prefetch=2, grid=(nb,),
            in_specs=[pl.BlockSpec((MOE_BLOCK, dh), lambda b, be, nu: (b, 0)),
                      pl.BlockSpec((None, None, d, 2 * f), lambda b, be, nu: (layer, be[b], 0, 0))],
            out_specs=pl.BlockSpec((MOE_BLOCK, f), lambda b, be, nu: (b, 0))),
        compiler_params=_cparams(1, 2 * (MOE_BLOCK * d * 4 + d * 2 * f * 2) + 3 * MOE_BLOCK * 2 * f * 4 + (4 << 20)),
        name="moe_up",
    )(block_exp, n_blocks_used, xs, w_gate_up)
    return pl.pallas_call(
        _moe_down_kernel,
        out_shape=jax.ShapeDtypeStruct((n_slots, dh), jnp.uint32),
        grid_spec=pltpu.PrefetchScalarGridSpec(
            num_scalar_prefetch=2, grid=(nb,),
            in_specs=[pl.BlockSpec((MOE_BLOCK, f), lambda b, be, nu: (b, 0)),
                      pl.BlockSpec((MOE_BLOCK, 1), lambda b, be, nu: (b, 0)),
                      pl.BlockSpec((None, None, f, d), lambda b, be, nu: (layer, be[b], 0, 0))],
            out_specs=pl.BlockSpec((MOE_BLOCK, dh), lambda b, be, nu: (b, 0))),
        compiler_params=_cparams(1, 2 * (f * d * 2 + MOE_BLOCK * d * 4) + 2 * MOE_BLOCK * d * 4 + (4 << 20)),
        name="moe_down",
    )(block_exp, n_blocks_used, hmid, slot_w.reshape(n_slots, 1), w_down)


def _moe_combine_kernel(pos_hbm, ys_hbm, x_ref, o_ref, idx_smem, ybuf, sem_idx, sem_rows, *, tm):
    i = pl.program_id(0)
    cp = pltpu.make_async_copy(pos_hbm.at[i], idx_smem, sem_idx)
    cp.start()
    cp.wait()

    def row_copy(r):
        return pltpu.make_async_copy(ys_hbm.at[pl.ds(idx_smem[0, r], 1), :], ybuf.at[pl.ds(r, 1), :], sem_rows)

    def start(r, carry):
        row_copy(r).start()
        return carry

    def wait(r, carry):
        row_copy(r).wait()
        return carry

    lax.fori_loop(0, 2 * tm, start, 0)
    lax.fori_loop(0, 2 * tm, wait, 0)
    a_hi, a_lo = _unpack_bf16_pairs(ybuf[0:tm, :])
    b_hi, b_lo = _unpack_bf16_pairs(ybuf[tm:2 * tm, :])
    half = a_hi.shape[1]
    o_ref[:, :half] = x_ref[:, :half] + (a_hi + b_hi)
    o_ref[:, half:] = x_ref[:, half:] + (a_lo + b_lo)


def _moe_combine(x, ys, pos):
    n, d = x.shape
    tm = _tile(n, MOE_BLOCK)
    nb = n // tm
    pos_tiles = pos.reshape(nb, tm, 2).transpose(0, 2, 1).reshape(nb, 1, 2 * tm)
    return pl.pallas_call(
        functools.partial(_moe_combine_kernel, tm=tm),
        out_shape=jax.ShapeDtypeStruct((n, d), x.dtype),
        grid=(nb,),
        in_specs=[pl.BlockSpec(memory_space=pl.ANY), pl.BlockSpec(memory_space=pl.ANY),
                  pl.BlockSpec((tm, d), lambda i: (i, 0))],
        out_specs=pl.BlockSpec((tm, d), lambda i: (i, 0)),
        scratch_shapes=[pltpu.SMEM((1, 2 * tm), jnp.int32),
                        pltpu.VMEM((2 * tm, d // 2), jnp.uint32),
                        pltpu.SemaphoreType.DMA, pltpu.SemaphoreType.DMA],
        compiler_params=_cparams(1, 6 * tm * d * 4 + (4 << 20)),
        name="moe_combine",
    )(pos_tiles, ys, x)


def _moe(x, gain, w_route, w_gate_up, w_down, layer):
    n, d = x.shape
    n_exp = N_EXPERT_GROUPS * EXPERTS_PER_GROUP
    h, eid, ew = _norm_route(x, gain, w_route)
    flat_e = eid[:, :2].reshape(-1)
    flat_w = ew[:, :2].reshape(-1)
    nk = 2 * n
    onehot = (flat_e[:, None] == jnp.arange(n_exp, dtype=jnp.int32)[None, :]).astype(jnp.int32)
    csum = jnp.cumsum(onehot, axis=0)
    rank = jnp.take_along_axis(csum, flat_e[:, None], axis=1)[:, 0] - 1
    counts = csum[-1]
    pcounts = (counts + MOE_BLOCK - 1) // MOE_BLOCK * MOE_BLOCK
    pend = jnp.cumsum(pcounts)
    pstart = pend - pcounts
    dest = pstart[flat_e] + rank
    n_blocks = -(-nk // MOE_BLOCK) + n_exp
    n_slots = n_blocks * MOE_BLOCK
    flat_tok = jnp.arange(nk, dtype=jnp.int32) // 2
    slot_tok = jnp.zeros((n_slots,), jnp.int32).at[dest].set(flat_tok)
    slot_w = jnp.zeros((n_slots,), jnp.float32).at[dest].set(flat_w)
    block_exp = jnp.minimum(jnp.searchsorted(pend, jnp.arange(n_blocks, dtype=jnp.int32) * MOE_BLOCK, side='right'),
                            n_exp - 1).astype(jnp.int32)
    n_used = (pend[-1] // MOE_BLOCK).astype(jnp.int32).reshape(1)
    xs = _row_gather(h, slot_tok, MOE_BLOCK)
    ys = _moe_experts(xs, slot_w, block_exp, n_used, w_gate_up, w_down, layer)
    return _moe_combine(x, ys, dest.reshape(n, 2))


def _alibi_slopes(n):
    return [2.0 ** (-8.0 * (i + 1) / n) for i in range(n)]


def _token_mixers(x, l, p, n_seq):
    n, d = x.shape
    s_len = n // n_seq
    bf16 = jnp.bfloat16
    n_ga = len(DIL_GROUPS)
    a_w = A_HEADS * HEAD_DIM
    a_cols = n_ga * a_w
    b_qk = B_HEADS * 2 * B_QK_DIM
    b_w = B_HEADS * 2 * B_QK_DIM
    in_cols = p['w_in'].shape[2]
    slopes = _alibi_slopes(n_ga * A_HEADS + B_HEADS)

    h = _rmsnorm(x, p['norm_mix'][l])
    proj = _matmul(h, p['w_in'], l, bf16, name="in_proj")
    gates = _matmul(h, p['w_gate'], l, bf16, sigmoid=True, name="gate_proj")
    proj3 = proj.reshape(n_seq, s_len, in_cols)

    outs, lses = [], []
    for g, (_, dilation) in enumerate(DIL_GROUPS):
        o, lse = _attn_a_group(proj3, p['qnorm_a'][l], p['knorm_a'][l], group=g, dilation=dilation,
                               slopes=tuple(slopes[g * A_HEADS:(g + 1) * A_HEADS]), in_cols=in_cols,
                               n_heads=A_HEADS)
        outs.append(o.reshape(n, a_w))
        lses.append(lse.reshape(n, LANES))
    oa = _combine_a(outs, lses, A_HEADS)

    lam_init = 0.8 - 0.6 * math.exp(-0.3 * l)
    lam_vecs = jnp.stack([p['lambda_q1'][l], p['lambda_k1'][l], p['lambda_q2'][l], p['lambda_k2'][l]])
    ob = _attn_b(proj3, p['qnorm_b'][l], p['knorm_b'][l], lam_vecs, p['subln_b'][l],
                 slopes[n_ga * A_HEADS:], lam_init=lam_init,
                 q_col=3 * a_cols, k_col=3 * a_cols + b_qk, v_col=3 * a_cols + 2 * b_qk,
                 n_heads=B_HEADS).reshape(n, b_w)

    c0 = 3 * a_cols + 2 * b_qk + b_w
    oc = _conv(proj3, p['conv_w'][l], u_col=c0, b_col=c0 + C_WIDTH, c_col=c0 + 2 * C_WIDTH).reshape(n, C_WIDTH)

    merged = _gated_proj(oa, ob, oc, gates, p['w_proj_a'], p['w_proj_b'], p['w_proj_c'], l)
    return _matmul(merged, p['w_out'], l, jnp.float32, residual=x, tn=512, name="out_proj")


def _route_weights(w_group, w_expert):
    d = w_group.shape[0]
    used = w_group.shape[1] + w_expert.shape[1]
    return jnp.concatenate([w_group, w_expert, jnp.zeros((d, LANES - used), w_group.dtype)],
                           axis=1).astype(jnp.bfloat16)


def kernel(x_prompt, x_sample, norm_mix, w_in, qnorm_a, knorm_a, qnorm_b, knorm_b, lambda_q1, lambda_k1,
           lambda_q2, lambda_k2, subln_b, conv_w, w_proj_a, w_proj_b, w_proj_c, w_gate, w_out, norm_ffn,
           w_route_group, w_route_expert, w_gate_up, w_down):
    bf16 = jnp.bfloat16
    p = dict(norm_mix=norm_mix, w_in=w_in.astype(bf16), qnorm_a=qnorm_a, knorm_a=knorm_a, qnorm_b=qnorm_b,
             knorm_b=knorm_b, lambda_q1=lambda_q1, lambda_k1=lambda_k1, lambda_q2=lambda_q2, lambda_k2=lambda_k2,
             subln_b=subln_b, conv_w=conv_w, w_proj_a=w_proj_a.astype(bf16), w_proj_b=w_proj_b.astype(bf16),
             w_proj_c=w_proj_c.astype(bf16), w_gate=w_gate.astype(bf16), w_out=w_out.astype(bf16))
    w_gate_up_b, w_down_b = w_gate_up.astype(bf16), w_down.astype(bf16)
    bp, s_len, d = x_prompt.shape
    bs = x_sample.shape[0]
    assert x_sample.shape[1:] == (s_len, d)
    n_seq = bp + bs
    x = jnp.concatenate([x_prompt.reshape(bp * s_len, d), x_sample.reshape(bs * s_len, d)], axis=0)
    for l in range(norm_mix.shape[0]):
        x = _token_mixers(x, l, p, n_seq)
        x = _moe(x, norm_ffn[l], _route_weights(w_route_group[l], w_route_expert[l]), w_gate_up_b, w_down_b, l)
    return (x[:bp * s_len].reshape(bp, s_len, d), x[bp * s_len:].reshape(bs, s_len, d))
```

```python
import functools
import math

import jax
import jax.numpy as jnp
from jax import lax
from jax.experimental import pallas as pl
from jax.experimental.pallas import tpu as pltpu

HEAD_DIM = 128
DIL_GROUPS = ((128, 1), (512, 4), (2048, 16))
A_HEADS = 8
B_HEADS = 8
B_QK_DIM = 64
C_WIDTH = 2048
N_EXPERT_GROUPS = 4
EXPERTS_PER_GROUP = 8
D_FF_EXPERT = 1024
MOE_BLOCK = 256
RMS_EPS = 1e-6
NEG_BIG = -1e30

LANES = 128
V7X_VMEM_LIMIT_BYTES = 56 * 1024 * 1024

A_HALF = DIL_GROUPS[0][0] // (2 * DIL_GROUPS[0][1])


def _cparams(n_grid_dims, vmem_bytes):
    return pltpu.CompilerParams(
        dimension_semantics=("arbitrary",) * n_grid_dims,
        vmem_limit_bytes=int(min(max(vmem_bytes, 16 * 1024 * 1024), V7X_VMEM_LIMIT_BYTES)),
    )


def _tile(n, pref):
    t = min(n, pref)
    while n % t:
        t //= 2
    return t


def _segments(x):
    return tuple(x) if isinstance(x, (tuple, list)) else (x,)


def _segment_specs(segs, tm, tn, row_axis, col_of):
    nb0 = segs[0].shape[0] // tm
    if len(segs) == 1:
        return [pl.BlockSpec((tm, tn), lambda *g: (g[row_axis], col_of(*g)))], nb0
    return [pl.BlockSpec((tm, tn), lambda *g: (jnp.minimum(g[row_axis], nb0 - 1), col_of(*g))),
            pl.BlockSpec((tm, tn), lambda *g: (jnp.maximum(g[row_axis] - nb0, 0), col_of(*g)))], nb0


def _pick_segment(refs, step, nb0):
    if len(refs) == 1:
        return refs[0][...]
    return jnp.where(step < nb0, refs[0][...], refs[1][...])


def _rmsnorm_kernel(*refs, nb0):
    *x_refs, g_ref, o_ref = refs
    x = _pick_segment(x_refs, pl.program_id(0), nb0)
    ms = jnp.mean(x * x, axis=-1, keepdims=True)
    o_ref[...] = (x * lax.rsqrt(ms + RMS_EPS) * g_ref[...]).astype(o_ref.dtype)


def _rmsnorm(x, gain):
    segs = _segments(x)
    n, d = sum(a.shape[0] for a in segs), segs[0].shape[1]
    tm = _tile(min(a.shape[0] for a in segs), 256)
    x_specs, nb0 = _segment_specs(segs, tm, d, 0, lambda i: 0)
    return pl.pallas_call(
        functools.partial(_rmsnorm_kernel, nb0=nb0),
        out_shape=jax.ShapeDtypeStruct((n, d), jnp.bfloat16),
        grid=(n // tm,),
        in_specs=x_specs + [pl.BlockSpec((1, d), lambda i: (0, 0))],
        out_specs=pl.BlockSpec((tm, d), lambda i: (i, 0)),
        compiler_params=_cparams(1, (2 * len(segs) + 2) * tm * d * 4),
        name="rmsnorm",
    )(*segs, gain.reshape(1, d))


def _matmul_kernel(x_ref, w_ref, *refs, sigmoid, res_nb0):
    *r_refs, o_ref = refs
    acc = jnp.dot(x_ref[...], w_ref[...], preferred_element_type=jnp.float32)
    if sigmoid:
        acc = 1.0 / (1.0 + jnp.exp(-acc))
    if r_refs:
        acc = acc + _pick_segment(r_refs, pl.program_id(1), res_nb0)
    o_ref[...] = acc.astype(o_ref.dtype)


def _matmul(x, w, layer, out_dtype, *, sigmoid=False, residual=None, tm=1024, tn=1024, name="matmul"):
    n, k = x.shape
    c = w.shape[2]
    tm, tn = _tile(n, tm), _tile(c, tn)
    in_specs = [pl.BlockSpec((tm, k), lambda j, i: (i, 0)),
                pl.BlockSpec((None, k, tn), lambda j, i: (layer, 0, j))]
    args = [x, w]
    out_bytes = jnp.dtype(out_dtype).itemsize
    vmem = 2 * (tm * k * 2 + k * tn * 2 + tm * tn * out_bytes) + 2 * tm * tn * 4
    res_nb0 = 0
    if residual is not None:
        segs = _segments(residual)
        r_specs, res_nb0 = _segment_specs(segs, tm, tn, 1, lambda j, i: j)
        in_specs += r_specs
        args += list(segs)
        vmem += 2 * len(segs) * tm * tn * 4
    return pl.pallas_call(
        functools.partial(_matmul_kernel, sigmoid=sigmoid, res_nb0=res_nb0),
        out_shape=jax.ShapeDtypeStruct((n, c), out_dtype),
        grid=(c // tn, n // tm),
        in_specs=in_specs,
        out_specs=pl.BlockSpec((tm, tn), lambda j, i: (i, j)),
        compiler_params=_cparams(2, vmem + (4 << 20)),
        name=name,
    )(*args)


def _head_rmsnorm(x, gain):
    ms = jnp.mean(x * x, axis=-1, keepdims=True)
    return x * lax.rsqrt(ms + RMS_EPS) * gain


def _attn_a_kernel(q_ref, kp_ref, km_ref, kn_ref, vp_ref, vm_ref, vn_ref, gq_ref, gk_ref,
                   o_ref, lse_ref, qn_scr, kn_scr, v_scr, *, dilation, slopes, tu, sub):
    half = A_HALF
    i = pl.program_id(2)
    n_i = pl.num_programs(2)
    n_heads = len(slopes)
    scale = HEAD_DIM ** -0.5

    for h in range(n_heads):
        cs = slice(h * HEAD_DIM, (h + 1) * HEAD_DIM)
        qn_scr[:, cs] = _head_rmsnorm(q_ref[:, cs].astype(jnp.float32), gq_ref[...]).astype(qn_scr.dtype)
        for off, ref, rows in ((0, kp_ref, half), (half, km_ref, tu), (half + tu, kn_ref, half)):
            kn_scr[off:off + rows, cs] = _head_rmsnorm(ref[:, cs].astype(jnp.float32),
                                                       gk_ref[...]).astype(kn_scr.dtype)
    v_scr[0:half, :] = vp_ref[...]
    v_scr[half:half + tu, :] = vm_ref[...]
    v_scr[half + tu:, :] = vn_ref[...]

    kw = sub + 2 * half
    ii = lax.broadcasted_iota(jnp.int32, (sub, kw), 0)
    jj = lax.broadcasted_iota(jnp.int32, (sub, kw), 1)
    rel = jnp.abs(jj - half - ii)
    band = rel <= half
    dist = (rel * dilation).astype(jnp.float32)
    lane = lax.broadcasted_iota(jnp.int32, (sub, LANES), 1)

    def body(t, carry):
        a = pl.multiple_of(t * sub, sub)
        kpos = a + jj
        valid = band & ((kpos >= half) | (i > 0)) & ((kpos < tu + half) | (i < n_i - 1))
        lse_tile = jnp.zeros((sub, LANES), jnp.float32)
        for h in range(n_heads):
            cs = slice(h * HEAD_DIM, (h + 1) * HEAD_DIM)
            qh = qn_scr[pl.ds(a, sub), cs]
            kh = kn_scr[pl.ds(a, kw), cs]
            vh = v_scr[pl.ds(a, kw), cs]
            s = lax.dot_general(qh, kh, (((1,), (1,)), ((), ())), preferred_element_type=jnp.float32)
            s = jnp.where(valid, s * scale - slopes[h] * dist, NEG_BIG)
            m = jnp.max(s, axis=-1, keepdims=True)
            p = jnp.exp(s - m)
            l = jnp.sum(p, axis=-1, keepdims=True)
            o = jnp.dot(p.astype(vh.dtype), vh, preferred_element_type=jnp.float32) / l
            o_ref[pl.ds(a, sub), cs] = o.astype(o_ref.dtype)
            lse_tile = jnp.where(lane == h, m + jnp.log(l), lse_tile)
        lse_ref[pl.ds(a, sub), :] = lse_tile
        return carry

    lax.fori_loop(0, tu // sub, body, 0)


def _attn_a_group(proj3, gq, gk, *, group, dilation, slopes, in_cols, n_heads):
    n_seq, s_len, _ = proj3.shape
    half = A_HALF
    u_len = s_len // dilation
    width = n_heads * HEAD_DIM
    a_cols = len(DIL_GROUPS) * width
    tu = _tile(u_len, 512)
    sub = min(tu, 128)
    q_blk = group
    k_blk = a_cols // width + group
    v_blk = 2 * a_cols // width + group
    if dilation > 1:
        proj3 = jnp.concatenate([proj3[:, :, c * width:(c + 1) * width] for c in (q_blk, k_blk, v_blk)], axis=-1)
        in_cols, q_blk, k_blk, v_blk = 3 * width, 0, 1, 2
    pv = proj3.reshape(n_seq, u_len, dilation * in_cols)
    blocks_per_row = in_cols // width
    nh = tu // half
    n_halo = u_len // half

    def main(col):
        return pl.BlockSpec((None, tu, width), lambda b, r, i: (b, i, r * blocks_per_row + col))

    def prev(col):
        return pl.BlockSpec((None, half, width),
                            lambda b, r, i: (b, jnp.maximum(i * nh - 1, 0), r * blocks_per_row + col))

    def nxt(col):
        return pl.BlockSpec((None, half, width),
                            lambda b, r, i: (b, jnp.minimum((i + 1) * nh, n_halo - 1), r * blocks_per_row + col))

    gspec = pl.BlockSpec((1, HEAD_DIM), lambda b, r, i: (0, 0))
    o, lse = pl.pallas_call(
        functools.partial(_attn_a_kernel, dilation=dilation, slopes=slopes, tu=tu, sub=sub),
        out_shape=(jax.ShapeDtypeStruct((n_seq, u_len, dilation * width), jnp.bfloat16),
                   jax.ShapeDtypeStruct((n_seq, u_len, dilation * LANES), jnp.float32)),
        grid=(n_seq, dilation, u_len // tu),
        in_specs=[main(q_blk), prev(k_blk), main(k_blk), nxt(k_blk),
                  prev(v_blk), main(v_blk), nxt(v_blk), gspec, gspec],
        out_specs=(pl.BlockSpec((None, tu, width), lambda b, r, i: (b, i, r)),
                   pl.BlockSpec((None, tu, LANES), lambda b, r, i: (b, i, r))),
        scratch_shapes=[pltpu.VMEM((tu, width), jnp.bfloat16),
                        pltpu.VMEM((tu + 2 * half, width), jnp.bfloat16),
                        pltpu.VMEM((tu + 2 * half, width), jnp.bfloat16)],
        compiler_params=_cparams(3, 32 << 20),
        name=f"attn_a_g{group}",
    )(pv, pv, pv, pv, pv, pv, pv, gq.reshape(1, HEAD_DIM), gk.reshape(1, HEAD_DIM))
    return o.reshape(n_seq, s_len, width), lse.reshape(n_seq, s_len, LANES)


def _combine_a_kernel(o0_ref, o1_ref, o2_ref, l0_ref, l1_ref, l2_ref, out_ref, *, n_heads):
    l0, l1, l2 = l0_ref[...], l1_ref[...], l2_ref[...]
    m = jnp.maximum(jnp.maximum(l0, l1), l2)
    e0, e1, e2 = jnp.exp(l0 - m), jnp.exp(l1 - m), jnp.exp(l2 - m)
    den = e0 + e1 + e2
    w0, w1, w2 = e0 / den, e1 / den, e2 / den
    for h in range(n_heads):
        cs = slice(h * HEAD_DIM, (h + 1) * HEAD_DIM)
        acc = (w0[:, h:h + 1] * o0_ref[:, cs].astype(jnp.float32)
               + w1[:, h:h + 1] * o1_ref[:, cs].astype(jnp.float32)
               + w2[:, h:h + 1] * o2_ref[:, cs].astype(jnp.float32))
        out_ref[:, cs] = acc.astype(out_ref.dtype)


def _combine_a(outs, lses, n_heads):
    n, width = outs[0].shape
    tm = _tile(n, 512)
    ospec = pl.BlockSpec((tm, width), lambda i: (i, 0))
    lspec = pl.BlockSpec((tm, LANES), lambda i: (i, 0))
    return pl.pallas_call(
        functools.partial(_combine_a_kernel, n_heads=n_heads),
        out_shape=jax.ShapeDtypeStruct((n, width), jnp.bfloat16),
        grid=(n // tm,),
        in_specs=[ospec, ospec, ospec, lspec, lspec, lspec],
        out_specs=ospec,
        compiler_params=_cparams(1, 24 << 20),
        name="combine_a",
    )(*outs, *lses)


_B_FEAT = 4
_B_PAD_ROWS = 16


def _split_hi_lo(x):
    hi = x.astype(jnp.bfloat16)
    lo = (x - hi.astype(jnp.float32)).astype(jnp.bfloat16)
    return hi.astype(jnp.float32), lo.astype(jnp.float32)


_LOG2E = 1.4426950408889634


def _attn_b_kernel(q_ref, k_ref, v_ref, gq_ref, gk_ref, lam_ref, gs_ref, slope_ref, o_ref,
                   k_scr, vt_scr, q_scr, s_scr, m_scr, acc_scr, *, lam_init, tq, tk, s_len):
    i = pl.program_id(2)
    dq = B_QK_DIM
    n_kt = s_len // tk
    slope = slope_ref[...] * _LOG2E
    slope1 = slope[:, 0:1]

    lane_q = lax.broadcasted_iota(jnp.int32, (tq, 2 * dq), 1)
    lane_k = lax.broadcasted_iota(jnp.int32, (tk, 2 * dq), 1)

    def norm_maps(x, gain, lane):
        in0 = lane < dq
        sq = x * x
        ms0 = jnp.sum(jnp.where(in0, sq, 0.0), axis=-1, keepdims=True) / dq
        ms1 = jnp.sum(jnp.where(in0, 0.0, sq), axis=-1, keepdims=True) / dq
        inv = jnp.where(in0, lax.rsqrt(ms0 + RMS_EPS), lax.rsqrt(ms1 + RMS_EPS))
        return x * inv * gain

    @pl.when(i == 0)
    def _():
        def kbody(t, carry):
            r0 = pl.multiple_of(t * tk, tk)
            kn = norm_maps(k_ref[pl.ds(r0, tk), :].astype(jnp.float32), gk_ref[...], lane_k)
            b = lax.broadcasted_iota(jnp.int32, (tk, 2 * dq), 0).astype(jnp.float32)
            l_hi, l_lo = _split_hi_lo(slope * b)
            r_hi, r_lo = _split_hi_lo(slope * (tk - 1 - b))
            for m in range(2):
                f0 = (1 - m) * dq
                feat = jnp.where(lane_k == f0, l_hi,
                       jnp.where(lane_k == f0 + 1, l_lo,
                       jnp.where(lane_k == f0 + 2, r_hi,
                       jnp.where(lane_k == f0 + 3, r_lo, 0.0))))
                own = (lane_k >= m * dq) & (lane_k < (m + 1) * dq)
                k_scr[m, pl.ds(r0, tk), :] = jnp.where(own, kn, feat).astype(k_scr.dtype)
            vt_scr[t, 0:2 * dq, :] = v_ref[pl.ds(r0, tk), :].astype(jnp.float32).T.astype(vt_scr.dtype)
            row = lax.broadcasted_iota(jnp.int32, (_B_PAD_ROWS, tk), 0)
            vt_scr[t, 2 * dq:, :] = jnp.where(row == 0, 1.0, 0.0).astype(vt_scr.dtype)
            return carry
        lax.fori_loop(0, n_kt, kbody, 0)

    qn = norm_maps(q_ref[...].astype(jnp.float32), gq_ref[...], lane_q) * (dq ** -0.5 * _LOG2E)
    for m in range(2):
        f0 = (1 - m) * dq
        own = (lane_q >= m * dq) & (lane_q < (m + 1) * dq)
        left = (lane_q == f0) | (lane_q == f0 + 1)
        right = (lane_q == f0 + 2) | (lane_q == f0 + 3)
        q_scr[m, 0] = jnp.where(own, qn, 0.0).astype(q_scr.dtype)
        q_scr[m, 1] = jnp.where(own, qn, jnp.where(left, 1.0, 0.0)).astype(q_scr.dtype)
        q_scr[m, 2] = jnp.where(own, qn, jnp.where(right, 1.0, 0.0)).astype(q_scr.dtype)

    m_scr[...] = jnp.full(m_scr.shape, NEG_BIG, jnp.float32)
    acc_scr[...] = jnp.zeros(acc_scr.shape, jnp.float32)

    q_pos = i * tq + lax.broadcasted_iota(jnp.int32, (1, tq), 1)

    i_kt = (i * tq) // tk

    def scores(j, s_ref):
        j = jnp.asarray(j, jnp.int32)
        r0 = pl.multiple_of(j * tk, tk)
        ver = jnp.where(j == i_kt, 0, jnp.where(j < i_kt, 1, 2))
        for m in range(2):
            s_ref[m] = lax.dot_general(k_scr[m, pl.ds(r0, tk), :], q_scr[m, ver], (((1,), (1,)), ((), ())),
                                       preferred_element_type=jnp.float32)

    def diagonal_bias(j, s_ref):
        @pl.when(j == i_kt)
        def _():
            a = lax.broadcasted_iota(jnp.int32, (tk, tq), 0)
            b = lax.broadcasted_iota(jnp.int32, (tk, tq), 1)
            bias = slope1 * jnp.abs(a - b - (i * tq - i_kt * tk)).astype(jnp.float32)
            for m in range(2):
                s_ref[m] = s_ref[m] - bias

    def softmax_pv(j, s_ref):
        j = jnp.asarray(j, jnp.int32)
        sgn = jnp.where(j < i_kt, 1, jnp.where(j > i_kt, -1, 0))
        off = jnp.where(j > i_kt, tk - 1, 0)
        shift = slope1 * (sgn * (j * tk - q_pos) - off).astype(jnp.float32)
        vt = vt_scr[j]
        for m in range(2):
            s = s_ref[m]
            m_old = m_scr[m]
            m_new = jnp.maximum(m_old, jnp.max(s, axis=0, keepdims=True) + shift)
            p = jnp.exp2(s - (m_new - shift))
            alpha = jnp.exp2(m_old - m_new)
            acc_scr[m] = alpha * acc_scr[m] + jnp.dot(vt, p.astype(vt.dtype),
                                                      preferred_element_type=jnp.float32)
            m_scr[m] = m_new

    s_a, s_b = s_scr.at[0], s_scr.at[1]
    scores(0, s_a)
    diagonal_bias(0, s_a)

    def pair(t, carry):
        j = 2 * t
        scores(j + 1, s_b)
        softmax_pv(j, s_a)
        diagonal_bias(j + 1, s_b)
        scores(j + 2, s_a)
        softmax_pv(j + 1, s_b)
        diagonal_bias(j + 2, s_a)
        return carry

    lax.fori_loop(0, n_kt // 2 - 1, pair, 0)
    scores(n_kt - 1, s_b)
    softmax_pv(n_kt - 2, s_a)
    diagonal_bias(n_kt - 1, s_b)
    softmax_pv(n_kt - 1, s_b)

    lam_v = lam_ref[...]
    lam = (jnp.exp(jnp.sum(lam_v[0:1] * lam_v[1:2], axis=-1, keepdims=True))
           - jnp.exp(jnp.sum(lam_v[2:3] * lam_v[3:4], axis=-1, keepdims=True)) + lam_init)
    hw = 2 * dq
    out = (acc_scr[0, 0:hw, :] * (1.0 / acc_scr[0, hw:hw + 1, :])
           - lam * (acc_scr[1, 0:hw, :] * (1.0 / acc_scr[1, hw:hw + 1, :])))
    ms = jnp.mean(out * out, axis=0, keepdims=True)
    out = out * lax.rsqrt(ms + RMS_EPS) * gs_ref[...] * (1.0 - lam_init)
    o_ref[...] = out.T.astype(o_ref.dtype)


def _attn_b(proj3, gq, gk, lam_vecs, subln, slopes, *, lam_init, q_col, k_col, v_col, n_heads):
    n_seq, s_len, _ = proj3.shape
    hw = 2 * B_QK_DIM
    tq, tk = _tile(s_len, 1024), _tile(s_len, 1024)
    assert (s_len // tk) % 2 == 0 and tk % tq == 0, "key tiles are pipelined in pairs; a query tile sits in one key tile"
    qb, kb, vb = q_col // hw, k_col // hw, v_col // hw
    gq2 = jnp.tile(gq, 2).reshape(1, hw)
    gk2 = jnp.tile(gk, 2).reshape(1, hw)
    slope_arr = jnp.broadcast_to(jnp.asarray(slopes, jnp.float32)[:, None, None], (n_heads, 1, LANES))
    small = lambda shape: pl.BlockSpec(shape, lambda b, h, i: (0, 0))
    return pl.pallas_call(
        functools.partial(_attn_b_kernel, lam_init=lam_init, tq=tq, tk=tk, s_len=s_len),
        out_shape=jax.ShapeDtypeStruct((n_seq, s_len, n_heads * hw), jnp.bfloat16),
        grid=(n_seq, n_heads, s_len // tq),
        in_specs=[pl.BlockSpec((None, tq, hw), lambda b, h, i: (b, i, qb + h)),
                  pl.BlockSpec((None, s_len, hw), lambda b, h, i: (b, 0, kb + h)),
                  pl.BlockSpec((None, s_len, hw), lambda b, h, i: (b, 0, vb + h)),
                  small((1, hw)), small((1, hw)), small((4, B_QK_DIM)), small((hw, 1)),
                  pl.BlockSpec((None, 1, LANES), lambda b, h, i: (h, 0, 0))],
        out_specs=pl.BlockSpec((None, tq, hw), lambda b, h, i: (b, i, h)),
        scratch_shapes=[pltpu.VMEM((2, s_len, hw), jnp.bfloat16),
                        pltpu.VMEM((s_len // tk, hw + _B_PAD_ROWS, tk), jnp.bfloat16),
                        pltpu.VMEM((2, 3, tq, hw), jnp.bfloat16),
                        pltpu.VMEM((2, 2, tk, tq), jnp.float32),
                        pltpu.VMEM((2, 1, tq), jnp.float32),
                        pltpu.VMEM((2, hw + _B_PAD_ROWS, tq), jnp.float32)],
        compiler_params=_cparams(3, V7X_VMEM_LIMIT_BYTES),
        name="attn_b",
    )(proj3, proj3, proj3, gq2, gk2, lam_vecs, subln.reshape(hw, 1), slope_arr)


_CONV_HALO = 16


def _conv_kernel(u_ref, b_ref, c_ref, up_ref, cp_ref, un_ref, cn_ref, w_ref, o_ref, *, tm):
    i = pl.program_id(1)
    n_i = pl.num_programs(1)
    f32 = jnp.float32
    v = c_ref[...].astype(f32) * u_ref[...].astype(f32)
    v_prev = (cp_ref[_CONV_HALO - 1:_CONV_HALO, :].astype(f32) * up_ref[_CONV_HALO - 1:_CONV_HALO, :].astype(f32))
    v_next = cn_ref[0:1, :].astype(f32) * un_ref[0:1, :].astype(f32)
    v_prev = jnp.where(i > 0, v_prev, 0.0)
    v_next = jnp.where(i < n_i - 1, v_next, 0.0)
    row = lax.broadcasted_iota(jnp.int32, v.shape, 0)
    down = jnp.where(row == 0, v_prev, pltpu.roll(v, 1, axis=0))
    up = jnp.where(row == tm - 1, v_next, pltpu.roll(v, tm - 1, axis=0))
    w = w_ref[...]
    y = down * w[0:1] + v * w[1:2] + up * w[2:3]
    o_ref[...] = (b_ref[...].astype(f32) * y).astype(o_ref.dtype)


def _conv(proj3, conv_w, *, u_col, b_col, c_col):
    n_seq, s_len, _ = proj3.shape
    cw = conv_w.shape[1]
    tm = _tile(s_len, 512)
    tc = _tile(cw, 1024)
    ncb = cw // tc
    nh = tm // _CONV_HALO
    n_halo = s_len // _CONV_HALO

    def main(col):
        return pl.BlockSpec((None, tm, tc), lambda b, i, c: (b, i, col // tc + c))

    def prev(col):
        return pl.BlockSpec((None, _CONV_HALO, tc),
                            lambda b, i, c: (b, jnp.maximum(i * nh - 1, 0), col // tc + c))

    def nxt(col):
        return pl.BlockSpec((None, _CONV_HALO, tc),
                            lambda b, i, c: (b, jnp.minimum((i + 1) * nh, n_halo - 1), col // tc + c))

    return pl.pallas_call(
        functools.partial(_conv_kernel, tm=tm),
        out_shape=jax.ShapeDtypeStruct((n_seq, s_len, cw), jnp.bfloat16),
        grid=(n_seq, s_len // tm, ncb),
        in_specs=[main(u_col), main(b_col), main(c_col), prev(u_col), prev(c_col), nxt(u_col), nxt(c_col),
                  pl.BlockSpec((3, tc), lambda b, i, c: (0, c))],
        out_specs=pl.BlockSpec((None, tm, tc), lambda b, i, c: (b, i, c)),
        compiler_params=_cparams(3, 32 << 20),
        name="short_conv",
    )(proj3, proj3, proj3, proj3, proj3, proj3, proj3, conv_w)


def _gated_proj_kernel(oa_ref, ob_ref, oc_ref, g0_ref, g1_ref, g2_ref, wa_ref, wb_ref, wc_ref, o_ref):
    f32 = jnp.float32
    acc = g0_ref[...].astype(f32) * jnp.dot(oa_ref[...], wa_ref[...], preferred_element_type=f32)
    acc += g1_ref[...].astype(f32) * jnp.dot(ob_ref[...], wb_ref[...], preferred_element_type=f32)
    acc += g2_ref[...].astype(f32) * jnp.dot(oc_ref[...], wc_ref[...], preferred_element_type=f32)
    o_ref[...] = acc.astype(o_ref.dtype)


def _gated_proj(oa, ob, oc, gates, wa, wb, wc, layer):
    n, d = oa.shape[0], wa.shape[2]
    tm, tn = _tile(n, 512), _tile(d, 1024)
    nj = d // tn
    act = lambda a: pl.BlockSpec((tm, a.shape[1]), lambda j, i: (i, 0))
    gate = lambda br: pl.BlockSpec((tm, tn), lambda j, i: (i, br * nj + j))
    wsp = lambda w: pl.BlockSpec((None, w.shape[1], tn), lambda j, i: (layer, 0, j))
    ka, kb, kc = oa.shape[1], ob.shape[1], oc.shape[1]
    vmem = 2 * 2 * (tm * (ka + kb + kc) + 3 * tm * tn + (ka + kb + kc) * tn + tm * tn) + 4 * tm * tn * 4
    return pl.pallas_call(
        _gated_proj_kernel,
        out_shape=jax.ShapeDtypeStruct((n, d), jnp.bfloat16),
        grid=(nj, n // tm),
        in_specs=[act(oa), act(ob), act(oc), gate(0), gate(1), gate(2), wsp(wa), wsp(wb), wsp(wc)],
        out_specs=pl.BlockSpec((tm, tn), lambda j, i: (i, j)),
        compiler_params=_cparams(2, vmem + (4 << 20)),
        name="gated_proj",
    )(oa, ob, oc, gates, gates, gates, wa, wb, wc)


def _pack_bf16_pairs(x):
    c = x.shape[1] // 2
    bits = lax.bitcast_convert_type(x.astype(jnp.bfloat16).astype(jnp.float32), jnp.uint32)
    return (bits[:, :c] & jnp.uint32(0xFFFF0000)) | (bits[:, c:] >> 16)


def _unpack_bf16_pairs(p):
    hi = lax.bitcast_convert_type(p & jnp.uint32(0xFFFF0000), jnp.float32)
    lo = lax.bitcast_convert_type(p << 16, jnp.float32)
    return hi, lo


def _norm_route_kernel(x_ref, g_ref, wr_ref, h_ref, eid_ref, ew_ref):
    x = x_ref[...]
    ms = jnp.mean(x * x, axis=-1, keepdims=True)
    h = x * lax.rsqrt(ms + RMS_EPS) * g_ref[...]
    h_ref[...] = _pack_bf16_pairs(h)
    logits = jnp.dot(h.astype(jnp.bfloat16), wr_ref[...], preferred_element_type=jnp.float32)
    ng, ne = N_EXPERT_GROUPS, EXPERTS_PER_GROUP
    lane = lax.broadcasted_iota(jnp.int32, logits.shape, 1)
    big = jnp.int32(1 << 20)
    is_g = lane < ng
    gl = jnp.where(is_g, logits, NEG_BIG)
    gmax = jnp.max(gl, axis=-1, keepdims=True)
    garg = jnp.min(jnp.where(is_g & (gl == gmax), lane, big), axis=-1, keepdims=True)
    g_w = 1.0 / jnp.sum(jnp.where(is_g, jnp.exp(gl - gmax), 0.0), axis=-1, keepdims=True)
    lo = ng + ne * garg
    sel = (lane >= lo) & (lane < lo + ne)
    el = jnp.where(sel, logits, NEG_BIG)
    m1 = jnp.max(el, axis=-1, keepdims=True)
    a1 = jnp.min(jnp.where(sel & (el == m1), lane, big), axis=-1, keepdims=True)
    el2 = jnp.where(lane == a1, NEG_BIG, el)
    m2 = jnp.max(el2, axis=-1, keepdims=True)
    a2 = jnp.min(jnp.where(sel & (lane != a1) & (el2 == m2), lane, big), axis=-1, keepdims=True)
    t = jnp.exp(m2 - m1)
    w1 = g_w / (1.0 + t)
    w2 = g_w * t / (1.0 + t)
    eid_ref[...] = jnp.where(lane == 0, a1 - ng, jnp.where(lane == 1, a2 - ng, 0))
    ew_ref[...] = jnp.where(lane == 0, w1, jnp.where(lane == 1, w2, 0.0))


def _norm_route(x, gain, w_route):
    n, d = x.shape
    tm = _tile(n, 256)
    return pl.pallas_call(
        _norm_route_kernel,
        out_shape=(jax.ShapeDtypeStruct((n, d // 2), jnp.uint32),
                   jax.ShapeDtypeStruct((n, LANES), jnp.int32),
                   jax.ShapeDtypeStruct((n, LANES), jnp.float32)),
        grid=(n // tm,),
        in_specs=[pl.BlockSpec((tm, d), lambda i: (i, 0)),
                  pl.BlockSpec((1, d), lambda i: (0, 0)),
                  pl.BlockSpec((d, LANES), lambda i: (0, 0))],
        out_specs=(pl.BlockSpec((tm, d // 2), lambda i: (i, 0)),
                   pl.BlockSpec((tm, LANES), lambda i: (i, 0)),
                   pl.BlockSpec((tm, LANES), lambda i: (i, 0))),
        compiler_params=_cparams(1, 6 * tm * d * 4 + (8 << 20)),
        name="norm_route",
    )(x, gain.reshape(1, d), w_route)


_DMA_ISSUE_UNROLL = 8


def _fetch_step_indices(idx_hbm, idx_smem, sem_idx):
    b = pl.program_id(0)
    slot = b % 2

    def idx_copy(step, s):
        return pltpu.make_async_copy(idx_hbm.at[step], idx_smem.at[s], sem_idx.at[s])

    @pl.when(b == 0)
    def _():
        idx_copy(0, 0).start()

    idx_copy(b, slot).wait()

    @pl.when(b + 1 < pl.num_programs(0))
    def _():
        idx_copy(b + 1, 1 - slot).start()

    return slot


def _gather_rows(src_hbm, dst_ref, idx_smem, slot, sem_rows, n_rows):
    def start(r, carry):
        pltpu.make_async_copy(src_hbm.at[pl.ds(idx_smem[slot, 0, r], 1), :], dst_ref.at[pl.ds(r, 1), :],
                              sem_rows).start()
        return carry

    lax.fori_loop(0, n_rows, start, 0, unroll=_DMA_ISSUE_UNROLL)
    pltpu.make_async_copy(src_hbm.at[pl.ds(0, n_rows), :], dst_ref, sem_rows).wait()


def _row_gather_kernel(idx_hbm, src_hbm, o_ref, idx_smem, sem_idx, sem_rows, *, rows):
    slot = _fetch_step_indices(idx_hbm, idx_smem, sem_idx)
    _gather_rows(src_hbm, o_ref, idx_smem, slot, sem_rows, rows)


def _row_gather(src, idx, rows):
    n_out = idx.shape[0]
    d = src.shape[1]
    nb = n_out // rows
    return pl.pallas_call(
        functools.partial(_row_gather_kernel, rows=rows),
        out_shape=jax.ShapeDtypeStruct((n_out, d), src.dtype),
        grid=(nb,),
        in_specs=[pl.BlockSpec(memory_space=pl.ANY), pl.BlockSpec(memory_space=pl.ANY)],
        out_specs=pl.BlockSpec((rows, d), lambda b: (b, 0)),
        scratch_shapes=[pltpu.SMEM((2, 1, rows), jnp.int32),
                        pltpu.SemaphoreType.DMA((2,)), pltpu.SemaphoreType.DMA],
        compiler_params=_cparams(1, 4 * rows * d * 4 + (4 << 20)),
        name="row_gather",
    )(idx.reshape(nb, 1, rows), src)


def _moe_up_kernel(be_ref, nb_ref, x_ref, w_ref, h_ref, *, f):
    b = pl.program_id(0)

    @pl.when(b < nb_ref[0])
    def _():
        x_hi, x_lo = _unpack_bf16_pairs(x_ref[...])
        half = x_hi.shape[1]
        gu = (jnp.dot(x_hi.astype(jnp.bfloat16), w_ref[:half, :], preferred_element_type=jnp.float32)
              + jnp.dot(x_lo.astype(jnp.bfloat16), w_ref[half:, :], preferred_element_type=jnp.float32))
        g, u = gu[:, :f], gu[:, f:]
        h_ref[...] = (g * (1.0 / (1.0 + jnp.exp(-g))) * u).astype(h_ref.dtype)

    @pl.when(b >= nb_ref[0])
    def _():
        h_ref[...] = jnp.zeros(h_ref.shape, h_ref.dtype)


def _moe_down_kernel(be_ref, nb_ref, h_ref, sw_ref, w_ref, y_ref):
    b = pl.program_id(0)

    @pl.when(b < nb_ref[0])
    def _():
        y = jnp.dot(h_ref[...], w_ref[...], preferred_element_type=jnp.float32)
        y_ref[...] = _pack_bf16_pairs(y * sw_ref[...])

    @pl.when(b >= nb_ref[0])
    def _():
        y_ref[...] = jnp.zeros(y_ref.shape, y_ref.dtype)


def _moe_experts(xs, slot_w, block_exp, n_blocks_used, w_gate_up, w_down, layer):
    n_slots, dh = xs.shape
    d = 2 * dh
    f = w_down.shape[2]
    nb = n_slots // MOE_BLOCK
    hmid = pl.pallas_call(
        functools.partial(_moe_up_kernel, f=f),
        out_shape=jax.ShapeDtypeStruct((n_slots, f), jnp.bfloat16),
        grid_spec=pltpu.PrefetchScalarGridSpec(
            num_scalar_prefetch=2, grid=(nb,),
            in_specs=[pl.BlockSpec((MOE_BLOCK, dh), lambda b, be, nu: (b, 0)),
                      pl.BlockSpec((None, None, d, 2 * f), lambda b, be, nu: (layer, be[b], 0, 0))],
            out_specs=pl.BlockSpec((MOE_BLOCK, f), lambda b, be, nu: (b, 0))),
        compiler_params=_cparams(1, 2 * (MOE_BLOCK * d * 4 + d * 2 * f * 2) + 3 * MOE_BLOCK * 2 * f * 4 + (4 << 20)),
        name="moe_up",
    )(block_exp, n_blocks_used, xs, w_gate_up)
    return pl.pallas_call(
        _moe_down_kernel,
        out_shape=jax.ShapeDtypeStruct((n_slots, dh), jnp.uint32),
        grid_spec=pltpu.PrefetchScalarGridSpec(
            num_scalar_prefetch=2, grid=(nb,),
            in_specs=[pl.BlockSpec((MOE_BLOCK, f), lambda b, be, nu: (b, 0)),
                      pl.BlockSpec((MOE_BLOCK, 1), lambda b, be, nu: (b, 0)),
                      pl.BlockSpec((None, None, f, d), lambda b, be, nu: (layer, be[b], 0, 0))],
            out_specs=pl.BlockSpec((MOE_BLOCK, dh), lambda b, be, nu: (b, 0))),
        compiler_params=_cparams(1, 2 * (f * d * 2 + MOE_BLOCK * d * 4) + 2 * MOE_BLOCK * d * 4 + (4 << 20)),
        name="moe_down",
    )(block_exp, n_blocks_used, hmid, slot_w.reshape(n_slots, 1), w_down)


def _moe_combine_kernel(pos_hbm, ys_hbm, x_ref, *refs, tm, nb0):
    *o_refs, idx_smem, ybuf, sem_idx, sem_rows = refs
    slot = _fetch_step_indices(pos_hbm, idx_smem, sem_idx)
    _gather_rows(ys_hbm, ybuf, idx_smem, slot, sem_rows, 2 * tm)
    a_hi, a_lo = _unpack_bf16_pairs(ybuf[0:tm, :])
    b_hi, b_lo = _unpack_bf16_pairs(ybuf[tm:2 * tm, :])
    half = a_hi.shape[1]
    out_hi = x_ref[:, :half] + (a_hi + b_hi)
    out_lo = x_ref[:, half:] + (a_lo + b_lo)
    if len(o_refs) == 1:
        o_refs[0][:, :half] = out_hi
        o_refs[0][:, half:] = out_lo
    else:
        for o_ref, mine in ((o_refs[0], pl.program_id(0) < nb0), (o_refs[1], pl.program_id(0) >= nb0)):
            @pl.when(mine)
            def _(o_ref=o_ref):
                o_ref[:, :half] = out_hi
                o_ref[:, half:] = out_lo


def _moe_combine(x, ys, pos, split_rows=None):
    n, d = x.shape
    tm = _tile(n if split_rows is None else math.gcd(split_rows, n - split_rows), MOE_BLOCK)
    nb = n // tm
    pos_tiles = pos.reshape(nb, tm, 2).transpose(0, 2, 1).reshape(nb, 1, 2 * tm)
    if split_rows is None:
        nb0 = nb
        out_shape = jax.ShapeDtypeStruct((n, d), x.dtype)
        out_specs = pl.BlockSpec((tm, d), lambda i: (i, 0))
    else:
        nb0 = split_rows // tm
        out_shape = (jax.ShapeDtypeStruct((split_rows, d), x.dtype), jax.ShapeDtypeStruct((n - split_rows, d), x.dtype))
        out_specs = (pl.BlockSpec((tm, d), lambda i: (jnp.minimum(i, nb0 - 1), 0)),
                     pl.BlockSpec((tm, d), lambda i: (jnp.maximum(i - nb0, 0), 0)))
    return pl.pallas_call(
        functools.partial(_moe_combine_kernel, tm=tm, nb0=nb0),
        out_shape=out_shape,
        grid=(nb,),
        in_specs=[pl.BlockSpec(memory_space=pl.ANY), pl.BlockSpec(memory_space=pl.ANY),
                  pl.BlockSpec((tm, d), lambda i: (i, 0))],
        out_specs=out_specs,
        scratch_shapes=[pltpu.SMEM((2, 1, 2 * tm), jnp.int32),
                        pltpu.VMEM((2 * tm, d // 2), jnp.uint32),
                        pltpu.SemaphoreType.DMA((2,)), pltpu.SemaphoreType.DMA],
        compiler_params=_cparams(1, 8 * tm * d * 4 + (4 << 20)),
        name="moe_combine",
    )(pos_tiles, ys, x)


def _moe(x, gain, w_route, w_gate_up, w_down, layer, split_rows=None):
    n, d = x.shape
    n_exp = N_EXPERT_GROUPS * EXPERTS_PER_GROUP
    h, eid, ew = _norm_route(x, gain, w_route)
    flat_e = eid[:, :2].reshape(-1)
    flat_w = ew[:, :2].reshape(-1)
    nk = 2 * n
    onehot = (flat_e[:, None] == jnp.arange(n_exp, dtype=jnp.int32)[None, :]).astype(jnp.int32)
    csum = jnp.cumsum(onehot, axis=0)
    rank = jnp.take_along_axis(csum, flat_e[:, None], axis=1)[:, 0] - 1
    counts = csum[-1]
    pcounts = (counts + MOE_BLOCK - 1) // MOE_BLOCK * MOE_BLOCK
    pend = jnp.cumsum(pcounts)
    pstart = pend - pcounts
    dest = pstart[flat_e] + rank
    n_blocks = -(-nk // MOE_BLOCK) + n_exp
    n_slots = n_blocks * MOE_BLOCK
    flat_tok = jnp.arange(nk, dtype=jnp.int32) // 2
    slot_tok = jnp.zeros((n_slots,), jnp.int32).at[dest].set(flat_tok)
    slot_w = jnp.zeros((n_slots,), jnp.float32).at[dest].set(flat_w)
    block_exp = jnp.minimum(jnp.searchsorted(pend, jnp.arange(n_blocks, dtype=jnp.int32) * MOE_BLOCK, side='right'),
                            n_exp - 1).astype(jnp.int32)
    n_used = (pend[-1] // MOE_BLOCK).astype(jnp.int32).reshape(1)
    xs = _row_gather(h, slot_tok, MOE_BLOCK)
    ys = _moe_experts(xs, slot_w, block_exp, n_used, w_gate_up, w_down, layer)
    return _moe_combine(x, ys, dest.reshape(n, 2), split_rows)


def _alibi_slopes(n):
    return [2.0 ** (-8.0 * (i + 1) / n) for i in range(n)]


def _token_mixers(x, l, p, n_seq):
    n, d = sum(a.shape[0] for a in _segments(x)), _segments(x)[0].shape[1]
    s_len = n // n_seq
    bf16 = jnp.bfloat16
    n_ga = len(DIL_GROUPS)
    a_w = A_HEADS * HEAD_DIM
    a_cols = n_ga * a_w
    b_qk = B_HEADS * 2 * B_QK_DIM
    b_w = B_HEADS * 2 * B_QK_DIM
    in_cols = p['w_in'].shape[2]
    slopes = _alibi_slopes(n_ga * A_HEADS + B_HEADS)

    h = _rmsnorm(x, p['norm_mix'][l])
    proj = _matmul(h, p['w_in'], l, bf16, name="in_proj")
    gates = _matmul(h, p['w_gate'], l, bf16, sigmoid=True, name="gate_proj")
    proj3 = proj.reshape(n_seq, s_len, in_cols)

    outs, lses = [], []
    for g, (_, dilation) in enumerate(DIL_GROUPS):
        o, lse = _attn_a_group(proj3, p['qnorm_a'][l], p['knorm_a'][l], group=g, dilation=dilation,
                               slopes=tuple(slopes[g * A_HEADS:(g + 1) * A_HEADS]), in_cols=in_cols,
                               n_heads=A_HEADS)
        outs.append(o.reshape(n, a_w))
        lses.append(lse.reshape(n, LANES))
    oa = _combine_a(outs, lses, A_HEADS)

    lam_init = 0.8 - 0.6 * math.exp(-0.3 * l)
    lam_vecs = jnp.stack([p['lambda_q1'][l], p['lambda_k1'][l], p['lambda_q2'][l], p['lambda_k2'][l]])
    ob = _attn_b(proj3, p['qnorm_b'][l], p['knorm_b'][l], lam_vecs, p['subln_b'][l],
                 slopes[n_ga * A_HEADS:], lam_init=lam_init,
                 q_col=3 * a_cols, k_col=3 * a_cols + b_qk, v_col=3 * a_cols + 2 * b_qk,
                 n_heads=B_HEADS).reshape(n, b_w)

    c0 = 3 * a_cols + 2 * b_qk + b_w
    oc = _conv(proj3, p['conv_w'][l], u_col=c0, b_col=c0 + C_WIDTH, c_col=c0 + 2 * C_WIDTH).reshape(n, C_WIDTH)

    merged = _gated_proj(oa, ob, oc, gates, p['w_proj_a'], p['w_proj_b'], p['w_proj_c'], l)
    return _matmul(merged, p['w_out'], l, jnp.float32, residual=x, tn=512, name="out_proj")


def _route_weights(w_group, w_expert):
    d = w_group.shape[0]
    used = w_group.shape[1] + w_expert.shape[1]
    return jnp.concatenate([w_group, w_expert, jnp.zeros((d, LANES - used), w_group.dtype)],
                           axis=1).astype(jnp.bfloat16)


def kernel(x_prompt, x_sample, norm_mix, w_in, qnorm_a, knorm_a, qnorm_b, knorm_b, lambda_q1, lambda_k1,
           lambda_q2, lambda_k2, subln_b, conv_w, w_proj_a, w_proj_b, w_proj_c, w_gate, w_out, norm_ffn,
           w_route_group, w_route_expert, w_gate_up, w_down):
    bf16 = jnp.bfloat16
    p = dict(norm_mix=norm_mix, w_in=w_in.astype(bf16), qnorm_a=qnorm_a, knorm_a=knorm_a, qnorm_b=qnorm_b,
             knorm_b=knorm_b, lambda_q1=lambda_q1, lambda_k1=lambda_k1, lambda_q2=lambda_q2, lambda_k2=lambda_k2,
             subln_b=subln_b, conv_w=conv_w, w_proj_a=w_proj_a.astype(bf16), w_proj_b=w_proj_b.astype(bf16),
             w_proj_c=w_proj_c.astype(bf16), w_gate=w_gate.astype(bf16), w_out=w_out.astype(bf16))
    w_gate_up_b, w_down_b = w_gate_up.astype(bf16), w_down.astype(bf16)
    bp, s_len, d = x_prompt.shape
    bs = x_sample.shape[0]
    assert x_sample.shape[1:] == (s_len, d)
    n_seq = bp + bs
    x = (x_prompt.reshape(bp * s_len, d), x_sample.reshape(bs * s_len, d))
    depth = norm_mix.shape[0]
    for l in range(depth):
        x = _token_mixers(x, l, p, n_seq)
        x = _moe(x, norm_ffn[l], _route_weights(w_route_group[l], w_route_expert[l]), w_gate_up_b, w_down_b, l,
                 split_rows=bp * s_len if l == depth - 1 else None)
    return (x[0].reshape(bp, s_len, d), x[1].reshape(bs, s_len, d))
```

```python
import functools
import math

import jax
import jax.numpy as jnp
from jax import lax
from jax.experimental import pallas as pl
from jax.experimental.pallas import tpu as pltpu

HEAD_DIM = 128
DIL_GROUPS = ((128, 1), (512, 4), (2048, 16))
A_HEADS = 8
B_HEADS = 8
B_QK_DIM = 64
C_WIDTH = 2048
N_EXPERT_GROUPS = 4
EXPERTS_PER_GROUP = 8
D_FF_EXPERT = 1024
MOE_BLOCK = 256
RMS_EPS = 1e-6
NEG_BIG = -1e30

LANES = 128
V7X_VMEM_LIMIT_BYTES = 56 * 1024 * 1024

A_HALF = DIL_GROUPS[0][0] // (2 * DIL_GROUPS[0][1])


def _cparams(n_grid_dims, vmem_bytes):
    return pltpu.CompilerParams(
        dimension_semantics=("arbitrary",) * n_grid_dims,
        vmem_limit_bytes=int(min(max(vmem_bytes, 16 * 1024 * 1024), V7X_VMEM_LIMIT_BYTES)),
    )


def _tile(n, pref):
    t = min(n, pref)
    while n % t:
        t //= 2
    return t


def _segments(x):
    return tuple(x) if isinstance(x, (tuple, list)) else (x,)


def _segment_specs(segs, tm, tn, row_axis, col_of):
    nb0 = segs[0].shape[0] // tm
    if len(segs) == 1:
        return [pl.BlockSpec((tm, tn), lambda *g: (g[row_axis], col_of(*g)))], nb0
    return [pl.BlockSpec((tm, tn), lambda *g: (jnp.minimum(g[row_axis], nb0 - 1), col_of(*g))),
            pl.BlockSpec((tm, tn), lambda *g: (jnp.maximum(g[row_axis] - nb0, 0), col_of(*g)))], nb0


def _pick_segment(refs, step, nb0):
    if len(refs) == 1:
        return refs[0][...]
    return jnp.where(step < nb0, refs[0][...], refs[1][...])


def _rmsnorm_kernel(*refs, nb0):
    *x_refs, g_ref, o_ref = refs
    x = _pick_segment(x_refs, pl.program_id(0), nb0)
    ms = jnp.mean(x * x, axis=-1, keepdims=True)
    o_ref[...] = (x * lax.rsqrt(ms + RMS_EPS) * g_ref[...]).astype(o_ref.dtype)


def _rmsnorm(x, gain):
    segs = _segments(x)
    n, d = sum(a.shape[0] for a in segs), segs[0].shape[1]
    tm = _tile(min(a.shape[0] for a in segs), 256)
    x_specs, nb0 = _segment_specs(segs, tm, d, 0, lambda i: 0)
    return pl.pallas_call(
        functools.partial(_rmsnorm_kernel, nb0=nb0),
        out_shape=jax.ShapeDtypeStruct((n, d), jnp.bfloat16),
        grid=(n // tm,),
        in_specs=x_specs + [pl.BlockSpec((1, d), lambda i: (0, 0))],
        out_specs=pl.BlockSpec((tm, d), lambda i: (i, 0)),
        compiler_params=_cparams(1, (2 * len(segs) + 2) * tm * d * 4),
        name="rmsnorm",
    )(*segs, gain.reshape(1, d))


def _matmul_kernel(x_ref, w_ref, *refs, sigmoid, res_nb0):
    *r_refs, o_ref = refs
    acc = jnp.dot(x_ref[...], w_ref[...], preferred_element_type=jnp.float32)
    if sigmoid:
        acc = 1.0 / (1.0 + jnp.exp(-acc))
    if r_refs:
        acc = acc + _pick_segment(r_refs, pl.program_id(1), res_nb0)
    o_ref[...] = acc.astype(o_ref.dtype)


def _matmul(x, w, layer, out_dtype, *, sigmoid=False, residual=None, tm=1024, tn=1024, name="matmul"):
    n, k = x.shape
    c = w.shape[2]
    tm, tn = _tile(n, tm), _tile(c, tn)
    in_specs = [pl.BlockSpec((tm, k), lambda j, i: (i, 0)),
                pl.BlockSpec((None, k, tn), lambda j, i: (layer, 0, j))]
    args = [x, w]
    out_bytes = jnp.dtype(out_dtype).itemsize
    vmem = 2 * (tm * k * 2 + k * tn * 2 + tm * tn * out_bytes) + 2 * tm * tn * 4
    res_nb0 = 0
    if residual is not None:
        segs = _segments(residual)
        r_specs, res_nb0 = _segment_specs(segs, tm, tn, 1, lambda j, i: j)
        in_specs += r_specs
        args += list(segs)
        vmem += 2 * len(segs) * tm * tn * 4
    return pl.pallas_call(
        functools.partial(_matmul_kernel, sigmoid=sigmoid, res_nb0=res_nb0),
        out_shape=jax.ShapeDtypeStruct((n, c), out_dtype),
        grid=(c // tn, n // tm),
        in_specs=in_specs,
        out_specs=pl.BlockSpec((tm, tn), lambda j, i: (i, j)),
        compiler_params=_cparams(2, vmem + (4 << 20)),
        name=name,
    )(*args)


def _head_rmsnorm(x, gain):
    ms = jnp.mean(x * x, axis=-1, keepdims=True)
    return x * lax.rsqrt(ms + RMS_EPS) * gain


def _attn_a_kernel(q_ref, kp_ref, km_ref, kn_ref, vp_ref, vm_ref, vn_ref, gq_ref, gk_ref,
                   o_ref, lse_ref, qn_scr, kn_scr, v_scr, *, dilation, slopes, tu, sub):
    half = A_HALF
    i = pl.program_id(2)
    n_i = pl.num_programs(2)
    n_heads = len(slopes)
    scale = HEAD_DIM ** -0.5

    for h in range(n_heads):
        cs = slice(h * HEAD_DIM, (h + 1) * HEAD_DIM)
        qn_scr[:, cs] = _head_rmsnorm(q_ref[:, cs].astype(jnp.float32), gq_ref[...]).astype(qn_scr.dtype)
        for off, ref, rows in ((0, kp_ref, half), (half, km_ref, tu), (half + tu, kn_ref, half)):
            kn_scr[off:off + rows, cs] = _head_rmsnorm(ref[:, cs].astype(jnp.float32),
                                                       gk_ref[...]).astype(kn_scr.dtype)
    v_scr[0:half, :] = vp_ref[...]
    v_scr[half:half + tu, :] = vm_ref[...]
    v_scr[half + tu:, :] = vn_ref[...]

    kw = sub + 2 * half
    ii = lax.broadcasted_iota(jnp.int32, (sub, kw), 0)
    jj = lax.broadcasted_iota(jnp.int32, (sub, kw), 1)
    rel = jnp.abs(jj - half - ii)
    band = rel <= half
    dist = (rel * dilation).astype(jnp.float32)
    lane = lax.broadcasted_iota(jnp.int32, (sub, LANES), 1)

    def body(t, carry):
        a = pl.multiple_of(t * sub, sub)
        kpos = a + jj
        valid = band & ((kpos >= half) | (i > 0)) & ((kpos < tu + half) | (i < n_i - 1))
        lse_tile = jnp.zeros((sub, LANES), jnp.float32)
        for h in range(n_heads):
            cs = slice(h * HEAD_DIM, (h + 1) * HEAD_DIM)
            qh = qn_scr[pl.ds(a, sub), cs]
            kh = kn_scr[pl.ds(a, kw), cs]
            vh = v_scr[pl.ds(a, kw), cs]
            s = lax.dot_general(qh, kh, (((1,), (1,)), ((), ())), preferred_element_type=jnp.float32)
            s = jnp.where(valid, s * scale - slopes[h] * dist, NEG_BIG)
            m = jnp.max(s, axis=-1, keepdims=True)
            p = jnp.exp(s - m)
            l = jnp.sum(p, axis=-1, keepdims=True)
            o = jnp.dot(p.astype(vh.dtype), vh, preferred_element_type=jnp.float32) / l
            o_ref[pl.ds(a, sub), cs] = o.astype(o_ref.dtype)
            lse_tile = jnp.where(lane == h, m + jnp.log(l), lse_tile)
        lse_ref[pl.ds(a, sub), :] = lse_tile
        return carry

    lax.fori_loop(0, tu // sub, body, 0)


def _deinterleave_kernel(*refs, dilation, width):
    *in_refs, o_ref, scr = refs
    per = in_refs[0].shape[0] // dilation
    n_lane_blocks = width // LANES
    for c, ref in enumerate(in_refs):
        for cc in range(n_lane_blocks):
            scr[cc] = ref[:, cc * LANES:(cc + 1) * LANES].astype(jnp.float32)
        for r in range(dilation):
            col = (r * len(in_refs) + c) * width
            for cc in range(n_lane_blocks):
                o_ref[:, col + cc * LANES:col + (cc + 1) * LANES] = (
                    scr[cc, pl.ds(r, per, stride=dilation), :].astype(o_ref.dtype))


def _deinterleave(proj3, col_blocks, width, dilation):
    n_seq, s_len, _ = proj3.shape
    rt = _tile(s_len, 1024)
    nc = len(col_blocks)
    return pl.pallas_call(
        functools.partial(_deinterleave_kernel, dilation=dilation, width=width),
        out_shape=jax.ShapeDtypeStruct((n_seq, s_len // dilation, dilation * nc * width), proj3.dtype),
        grid=(n_seq, s_len // rt),
        in_specs=[pl.BlockSpec((None, rt, width), lambda b, t, cb=cb: (b, t, cb)) for cb in col_blocks],
        out_specs=pl.BlockSpec((None, rt // dilation, dilation * nc * width), lambda b, t: (b, t, 0)),
        scratch_shapes=[pltpu.VMEM((width // LANES, rt, LANES), jnp.float32)],
        compiler_params=_cparams(2, V7X_VMEM_LIMIT_BYTES),
        name=f"deinterleave_d{dilation}",
    )(*([proj3] * nc))


def _attn_a_group(proj3, gq, gk, *, group, dilation, slopes, in_cols, n_heads):
    n_seq, s_len, _ = proj3.shape
    half = A_HALF
    u_len = s_len // dilation
    width = n_heads * HEAD_DIM
    a_cols = len(DIL_GROUPS) * width
    tu = _tile(u_len, 512)
    sub = min(tu, 128)
    q_blk = group
    k_blk = a_cols // width + group
    v_blk = 2 * a_cols // width + group
    if dilation > 1:
        pv = _deinterleave(proj3, (q_blk, k_blk, v_blk), width, dilation)
        in_cols, q_blk, k_blk, v_blk = 3 * width, 0, 1, 2
    else:
        pv = proj3
    blocks_per_row = in_cols // width
    nh = tu // half
    n_halo = u_len // half

    def main(col):
        return pl.BlockSpec((None, tu, width), lambda b, r, i: (b, i, r * blocks_per_row + col))

    def prev(col):
        return pl.BlockSpec((None, half, width),
                            lambda b, r, i: (b, jnp.maximum(i * nh - 1, 0), r * blocks_per_row + col))

    def nxt(col):
        return pl.BlockSpec((None, half, width),
                            lambda b, r, i: (b, jnp.minimum((i + 1) * nh, n_halo - 1), r * blocks_per_row + col))

    gspec = pl.BlockSpec((1, HEAD_DIM), lambda b, r, i: (0, 0))
    o, lse = pl.pallas_call(
        functools.partial(_attn_a_kernel, dilation=dilation, slopes=slopes, tu=tu, sub=sub),
        out_shape=(jax.ShapeDtypeStruct((n_seq, u_len, dilation * width), jnp.bfloat16),
                   jax.ShapeDtypeStruct((n_seq, u_len, dilation * LANES), jnp.float32)),
        grid=(n_seq, dilation, u_len // tu),
        in_specs=[main(q_blk), prev(k_blk), main(k_blk), nxt(k_blk),
                  prev(v_blk), main(v_blk), nxt(v_blk), gspec, gspec],
        out_specs=(pl.BlockSpec((None, tu, width), lambda b, r, i: (b, i, r)),
                   pl.BlockSpec((None, tu, LANES), lambda b, r, i: (b, i, r))),
        scratch_shapes=[pltpu.VMEM((tu, width), jnp.bfloat16),
                        pltpu.VMEM((tu + 2 * half, width), jnp.bfloat16),
                        pltpu.VMEM((tu + 2 * half, width), jnp.bfloat16)],
        compiler_params=_cparams(3, 32 << 20),
        name=f"attn_a_g{group}",
    )(pv, pv, pv, pv, pv, pv, pv, gq.reshape(1, HEAD_DIM), gk.reshape(1, HEAD_DIM))
    return o.reshape(n_seq, s_len, width), lse.reshape(n_seq, s_len, LANES)


def _combine_a_kernel(o0_ref, o1_ref, o2_ref, l0_ref, l1_ref, l2_ref, out_ref, *, n_heads):
    l0, l1, l2 = l0_ref[...], l1_ref[...], l2_ref[...]
    m = jnp.maximum(jnp.maximum(l0, l1), l2)
    e0, e1, e2 = jnp.exp(l0 - m), jnp.exp(l1 - m), jnp.exp(l2 - m)
    den = e0 + e1 + e2
    w0, w1, w2 = e0 / den, e1 / den, e2 / den
    for h in range(n_heads):
        cs = slice(h * HEAD_DIM, (h + 1) * HEAD_DIM)
        acc = (w0[:, h:h + 1] * o0_ref[:, cs].astype(jnp.float32)
               + w1[:, h:h + 1] * o1_ref[:, cs].astype(jnp.float32)
               + w2[:, h:h + 1] * o2_ref[:, cs].astype(jnp.float32))
        out_ref[:, cs] = acc.astype(out_ref.dtype)


def _combine_a(outs, lses, n_heads):
    n, width = outs[0].shape
    tm = _tile(n, 512)
    ospec = pl.BlockSpec((tm, width), lambda i: (i, 0))
    lspec = pl.BlockSpec((tm, LANES), lambda i: (i, 0))
    return pl.pallas_call(
        functools.partial(_combine_a_kernel, n_heads=n_heads),
        out_shape=jax.ShapeDtypeStruct((n, width), jnp.bfloat16),
        grid=(n // tm,),
        in_specs=[ospec, ospec, ospec, lspec, lspec, lspec],
        out_specs=ospec,
        compiler_params=_cparams(1, 24 << 20),
        name="combine_a",
    )(*outs, *lses)


_B_FEAT = 4
_B_PAD_ROWS = 16


def _split_hi_lo(x):
    hi = x.astype(jnp.bfloat16)
    lo = (x - hi.astype(jnp.float32)).astype(jnp.bfloat16)
    return hi.astype(jnp.float32), lo.astype(jnp.float32)


_LOG2E = 1.4426950408889634


def _attn_b_kernel(q_ref, k_ref, v_ref, gq_ref, gk_ref, lam_ref, gs_ref, slope_ref, o_ref,
                   k_scr, vt_scr, q_scr, s_scr, m_scr, acc_scr, *, lam_init, tq, tk, s_len):
    i = pl.program_id(2)
    dq = B_QK_DIM
    n_kt = s_len // tk
    slope = slope_ref[...] * _LOG2E
    slope1 = slope[:, 0:1]

    lane_q = lax.broadcasted_iota(jnp.int32, (tq, 2 * dq), 1)
    lane_k = lax.broadcasted_iota(jnp.int32, (tk, 2 * dq), 1)

    def norm_maps(x, gain, lane):
        in0 = lane < dq
        sq = x * x
        ms0 = jnp.sum(jnp.where(in0, sq, 0.0), axis=-1, keepdims=True) / dq
        ms1 = jnp.sum(jnp.where(in0, 0.0, sq), axis=-1, keepdims=True) / dq
        inv = jnp.where(in0, lax.rsqrt(ms0 + RMS_EPS), lax.rsqrt(ms1 + RMS_EPS))
        return x * inv * gain

    @pl.when(i == 0)
    def _():
        def kbody(t, carry):
            r0 = pl.multiple_of(t * tk, tk)
            kn = norm_maps(k_ref[pl.ds(r0, tk), :].astype(jnp.float32), gk_ref[...], lane_k)
            b = lax.broadcasted_iota(jnp.int32, (tk, 2 * dq), 0).astype(jnp.float32)
            l_hi, l_lo = _split_hi_lo(slope * b)
            r_hi, r_lo = _split_hi_lo(slope * (tk - 1 - b))
            for m in range(2):
                f0 = (1 - m) * dq
                feat = jnp.where(lane_k == f0, l_hi,
                       jnp.where(lane_k == f0 + 1, l_lo,
                       jnp.where(lane_k == f0 + 2, r_hi,
                       jnp.where(lane_k == f0 + 3, r_lo, 0.0))))
                own = (lane_k >= m * dq) & (lane_k < (m + 1) * dq)
                k_scr[m, pl.ds(r0, tk), :] = jnp.where(own, kn, feat).astype(k_scr.dtype)
            vt_scr[t, 0:2 * dq, :] = v_ref[pl.ds(r0, tk), :].astype(jnp.float32).T.astype(vt_scr.dtype)
            row = lax.broadcasted_iota(jnp.int32, (_B_PAD_ROWS, tk), 0)
            vt_scr[t, 2 * dq:, :] = jnp.where(row == 0, 1.0, 0.0).astype(vt_scr.dtype)
            return carry
        lax.fori_loop(0, n_kt, kbody, 0)

    qn = norm_maps(q_ref[...].astype(jnp.float32), gq_ref[...], lane_q) * (dq ** -0.5 * _LOG2E)
    for m in range(2):
        f0 = (1 - m) * dq
        own = (lane_q >= m * dq) & (lane_q < (m + 1) * dq)
        left = (lane_q == f0) | (lane_q == f0 + 1)
        right = (lane_q == f0 + 2) | (lane_q == f0 + 3)
        q_scr[m, 0] = jnp.where(own, qn, 0.0).astype(q_scr.dtype)
        q_scr[m, 1] = jnp.where(own, qn, jnp.where(left, 1.0, 0.0)).astype(q_scr.dtype)
        q_scr[m, 2] = jnp.where(own, qn, jnp.where(right, 1.0, 0.0)).astype(q_scr.dtype)

    m_scr[...] = jnp.full(m_scr.shape, NEG_BIG, jnp.float32)
    acc_scr[...] = jnp.zeros(acc_scr.shape, jnp.float32)

    q_pos = i * tq + lax.broadcasted_iota(jnp.int32, (1, tq), 1)

    i_kt = (i * tq) // tk

    def scores(j, s_ref):
        j = jnp.asarray(j, jnp.int32)
        r0 = pl.multiple_of(j * tk, tk)
        ver = jnp.where(j == i_kt, 0, jnp.where(j < i_kt, 1, 2))
        for m in range(2):
            s_ref[m] = lax.dot_general(k_scr[m, pl.ds(r0, tk), :], q_scr[m, ver], (((1,), (1,)), ((), ())),
                                       preferred_element_type=jnp.float32)

    def diagonal_bias(j, s_ref):
        @pl.when(j == i_kt)
        def _():
            a = lax.broadcasted_iota(jnp.int32, (tk, tq), 0)
            b = lax.broadcasted_iota(jnp.int32, (tk, tq), 1)
            bias = slope1 * jnp.abs(a - b - (i * tq - i_kt * tk)).astype(jnp.float32)
            for m in range(2):
                s_ref[m] = s_ref[m] - bias

    def softmax_pv(j, s_ref):
        j = jnp.asarray(j, jnp.int32)
        sgn = jnp.where(j < i_kt, 1, jnp.where(j > i_kt, -1, 0))
        off = jnp.where(j > i_kt, tk - 1, 0)
        shift = slope1 * (sgn * (j * tk - q_pos) - off).astype(jnp.float32)
        vt = vt_scr[j]
        for m in range(2):
            s = s_ref[m]
            m_old = m_scr[m]
            m_new = jnp.maximum(m_old, jnp.max(s, axis=0, keepdims=True) + shift)
            p = jnp.exp2(s - (m_new - shift))
            alpha = jnp.exp2(m_old - m_new)
            acc_scr[m] = alpha * acc_scr[m] + jnp.dot(vt, p.astype(vt.dtype),
                                                      preferred_element_type=jnp.float32)
            m_scr[m] = m_new

    s_a, s_b = s_scr.at[0], s_scr.at[1]
    scores(0, s_a)
    diagonal_bias(0, s_a)

    def pair(t, carry):
        j = 2 * t
        scores(j + 1, s_b)
        softmax_pv(j, s_a)
        diagonal_bias(j + 1, s_b)
        scores(j + 2, s_a)
        softmax_pv(j + 1, s_b)
        diagonal_bias(j + 2, s_a)
        return carry

    lax.fori_loop(0, n_kt // 2 - 1, pair, 0)
    scores(n_kt - 1, s_b)
    softmax_pv(n_kt - 2, s_a)
    diagonal_bias(n_kt - 1, s_b)
    softmax_pv(n_kt - 1, s_b)

    lam_v = lam_ref[...]
    lam = (jnp.exp(jnp.sum(lam_v[0:1] * lam_v[1:2], axis=-1, keepdims=True))
           - jnp.exp(jnp.sum(lam_v[2:3] * lam_v[3:4], axis=-1, keepdims=True)) + lam_init)
    hw = 2 * dq
    out = (acc_scr[0, 0:hw, :] * (1.0 / acc_scr[0, hw:hw + 1, :])
           - lam * (acc_scr[1, 0:hw, :] * (1.0 / acc_scr[1, hw:hw + 1, :])))
    ms = jnp.mean(out * out, axis=0, keepdims=True)
    out = out * lax.rsqrt(ms + RMS_EPS) * gs_ref[...] * (1.0 - lam_init)
    o_ref[...] = out.T.astype(o_ref.dtype)


def _attn_b(proj3, gq, gk, lam_vecs, subln, slopes, *, lam_init, q_col, k_col, v_col, n_heads):
    n_seq, s_len, _ = proj3.shape
    hw = 2 * B_QK_DIM
    tq, tk = _tile(s_len, 1024), _tile(s_len, 1024)
    assert (s_len // tk) % 2 == 0 and tk % tq == 0, "key tiles are pipelined in pairs; a query tile sits in one key tile"
    qb, kb, vb = q_col // hw, k_col // hw, v_col // hw
    gq2 = jnp.tile(gq, 2).reshape(1, hw)
    gk2 = jnp.tile(gk, 2).reshape(1, hw)
    slope_arr = jnp.broadcast_to(jnp.asarray(slopes, jnp.float32)[:, None, None], (n_heads, 1, LANES))
    small = lambda shape: pl.BlockSpec(shape, lambda b, h, i: (0, 0))
    return pl.pallas_call(
        functools.partial(_attn_b_kernel, lam_init=lam_init, tq=tq, tk=tk, s_len=s_len),
        out_shape=jax.ShapeDtypeStruct((n_seq, s_len, n_heads * hw), jnp.bfloat16),
        grid=(n_seq, n_heads, s_len // tq),
        in_specs=[pl.BlockSpec((None, tq, hw), lambda b, h, i: (b, i, qb + h)),
                  pl.BlockSpec((None, s_len, hw), lambda b, h, i: (b, 0, kb + h)),
                  pl.BlockSpec((None, s_len, hw), lambda b, h, i: (b, 0, vb + h)),
                  small((1, hw)), small((1, hw)), small((4, B_QK_DIM)), small((hw, 1)),
                  pl.BlockSpec((None, 1, LANES), lambda b, h, i: (h, 0, 0))],
        out_specs=pl.BlockSpec((None, tq, hw), lambda b, h, i: (b, i, h)),
        scratch_shapes=[pltpu.VMEM((2, s_len, hw), jnp.bfloat16),
                        pltpu.VMEM((s_len // tk, hw + _B_PAD_ROWS, tk), jnp.bfloat16),
                        pltpu.VMEM((2, 3, tq, hw), jnp.bfloat16),
                        pltpu.VMEM((2, 2, tk, tq), jnp.float32),
                        pltpu.VMEM((2, 1, tq), jnp.float32),
                        pltpu.VMEM((2, hw + _B_PAD_ROWS, tq), jnp.float32)],
        compiler_params=_cparams(3, V7X_VMEM_LIMIT_BYTES),
        name="attn_b",
    )(proj3, proj3, proj3, gq2, gk2, lam_vecs, subln.reshape(hw, 1), slope_arr)


_CONV_HALO = 16


def _conv_kernel(u_ref, b_ref, c_ref, up_ref, cp_ref, un_ref, cn_ref, w_ref, o_ref, *, tm):
    i = pl.program_id(1)
    n_i = pl.num_programs(1)
    f32 = jnp.float32
    v = c_ref[...].astype(f32) * u_ref[...].astype(f32)
    v_prev = (cp_ref[_CONV_HALO - 1:_CONV_HALO, :].astype(f32) * up_ref[_CONV_HALO - 1:_CONV_HALO, :].astype(f32))
    v_next = cn_ref[0:1, :].astype(f32) * un_ref[0:1, :].astype(f32)
    v_prev = jnp.where(i > 0, v_prev, 0.0)
    v_next = jnp.where(i < n_i - 1, v_next, 0.0)
    row = lax.broadcasted_iota(jnp.int32, v.shape, 0)
    down = jnp.where(row == 0, v_prev, pltpu.roll(v, 1, axis=0))
    up = jnp.where(row == tm - 1, v_next, pltpu.roll(v, tm - 1, axis=0))
    w = w_ref[...]
    y = down * w[0:1] + v * w[1:2] + up * w[2:3]
    o_ref[...] = (b_ref[...].astype(f32) * y).astype(o_ref.dtype)


def _conv(proj3, conv_w, *, u_col, b_col, c_col):
    n_seq, s_len, _ = proj3.shape
    cw = conv_w.shape[1]
    tm = _tile(s_len, 512)
    tc = _tile(cw, 1024)
    ncb = cw // tc
    nh = tm // _CONV_HALO
    n_halo = s_len // _CONV_HALO

    def main(col):
        return pl.BlockSpec((None, tm, tc), lambda b, i, c: (b, i, col // tc + c))

    def prev(col):
        return pl.BlockSpec((None, _CONV_HALO, tc),
                            lambda b, i, c: (b, jnp.maximum(i * nh - 1, 0), col // tc + c))

    def nxt(col):
        return pl.BlockSpec((None, _CONV_HALO, tc),
                            lambda b, i, c: (b, jnp.minimum((i + 1) * nh, n_halo - 1), col // tc + c))

    return pl.pallas_call(
        functools.partial(_conv_kernel, tm=tm),
        out_shape=jax.ShapeDtypeStruct((n_seq, s_len, cw), jnp.bfloat16),
        grid=(n_seq, s_len // tm, ncb),
        in_specs=[main(u_col), main(b_col), main(c_col), prev(u_col), prev(c_col), nxt(u_col), nxt(c_col),
                  pl.BlockSpec((3, tc), lambda b, i, c: (0, c))],
        out_specs=pl.BlockSpec((None, tm, tc), lambda b, i, c: (b, i, c)),
        compiler_params=_cparams(3, 32 << 20),
        name="short_conv",
    )(proj3, proj3, proj3, proj3, proj3, proj3, proj3, conv_w)


def _gated_proj_kernel(oa_ref, ob_ref, oc_ref, g0_ref, g1_ref, g2_ref, wa_ref, wb_ref, wc_ref, o_ref):
    f32 = jnp.float32
    acc = g0_ref[...].astype(f32) * jnp.dot(oa_ref[...], wa_ref[...], preferred_element_type=f32)
    acc += g1_ref[...].astype(f32) * jnp.dot(ob_ref[...], wb_ref[...], preferred_element_type=f32)
    acc += g2_ref[...].astype(f32) * jnp.dot(oc_ref[...], wc_ref[...], preferred_element_type=f32)
    o_ref[...] = acc.astype(o_ref.dtype)


def _gated_proj(oa, ob, oc, gates, wa, wb, wc, layer):
    n, d = oa.shape[0], wa.shape[2]
    tm, tn = _tile(n, 512), _tile(d, 1024)
    nj = d // tn
    act = lambda a: pl.BlockSpec((tm, a.shape[1]), lambda j, i: (i, 0))
    gate = lambda br: pl.BlockSpec((tm, tn), lambda j, i: (i, br * nj + j))
    wsp = lambda w: pl.BlockSpec((None, w.shape[1], tn), lambda j, i: (layer, 0, j))
    ka, kb, kc = oa.shape[1], ob.shape[1], oc.shape[1]
    vmem = 2 * 2 * (tm * (ka + kb + kc) + 3 * tm * tn + (ka + kb + kc) * tn + tm * tn) + 4 * tm * tn * 4
    return pl.pallas_call(
        _gated_proj_kernel,
        out_shape=jax.ShapeDtypeStruct((n, d), jnp.bfloat16),
        grid=(nj, n // tm),
        in_specs=[act(oa), act(ob), act(oc), gate(0), gate(1), gate(2), wsp(wa), wsp(wb), wsp(wc)],
        out_specs=pl.BlockSpec((tm, tn), lambda j, i: (i, j)),
        compiler_params=_cparams(2, vmem + (4 << 20)),
        name="gated_proj",
    )(oa, ob, oc, gates, gates, gates, wa, wb, wc)


def _pack_bf16_pairs(x):
    c = x.shape[1] // 2
    bits = lax.bitcast_convert_type(x.astype(jnp.bfloat16).astype(jnp.float32), jnp.uint32)
    return (bits[:, :c] & jnp.uint32(0xFFFF0000)) | (bits[:, c:] >> 16)


def _store_row_slabs(ref, packed):
    for c in range(ref.shape[1]):
        ref[:, c, :] = packed[:, c * LANES:(c + 1) * LANES]


def _load_row_slabs(ref, lo, rows):
    return jnp.concatenate([ref[lo:lo + rows, c, :] for c in range(ref.shape[1])], axis=1)


def _unpack_bf16_pairs(p):
    hi = lax.bitcast_convert_type(p & jnp.uint32(0xFFFF0000), jnp.float32)
    lo = lax.bitcast_convert_type(p << 16, jnp.float32)
    return hi, lo


def _norm_route_kernel(x_ref, g_ref, wr_ref, h_ref, eid_ref, ew_ref):
    x = x_ref[...]
    ms = jnp.mean(x * x, axis=-1, keepdims=True)
    h = x * lax.rsqrt(ms + RMS_EPS) * g_ref[...]
    _store_row_slabs(h_ref, _pack_bf16_pairs(h))
    logits = jnp.dot(h.astype(jnp.bfloat16), wr_ref[...], preferred_element_type=jnp.float32)
    ng, ne = N_EXPERT_GROUPS, EXPERTS_PER_GROUP
    lane = lax.broadcasted_iota(jnp.int32, logits.shape, 1)
    big = jnp.int32(1 << 20)
    is_g = lane < ng
    gl = jnp.where(is_g, logits, NEG_BIG)
    gmax = jnp.max(gl, axis=-1, keepdims=True)
    garg = jnp.min(jnp.where(is_g & (gl == gmax), lane, big), axis=-1, keepdims=True)
    g_w = 1.0 / jnp.sum(jnp.where(is_g, jnp.exp(gl - gmax), 0.0), axis=-1, keepdims=True)
    lo = ng + ne * garg
    sel = (lane >= lo) & (lane < lo + ne)
    el = jnp.where(sel, logits, NEG_BIG)
    m1 = jnp.max(el, axis=-1, keepdims=True)
    a1 = jnp.min(jnp.where(sel & (el == m1), lane, big), axis=-1, keepdims=True)
    el2 = jnp.where(lane == a1, NEG_BIG, el)
    m2 = jnp.max(el2, axis=-1, keepdims=True)
    a2 = jnp.min(jnp.where(sel & (lane != a1) & (el2 == m2), lane, big), axis=-1, keepdims=True)
    t = jnp.exp(m2 - m1)
    w1 = g_w / (1.0 + t)
    w2 = g_w * t / (1.0 + t)
    eid_ref[...] = jnp.where(lane == 0, a1 - ng, jnp.where(lane == 1, a2 - ng, 0))
    ew_ref[...] = jnp.where(lane == 0, w1, jnp.where(lane == 1, w2, 0.0))


def _norm_route(x, gain, w_route):
    n, d = x.shape
    tm = _tile(n, 256)
    return pl.pallas_call(
        _norm_route_kernel,
        out_shape=(jax.ShapeDtypeStruct((n, d // 2 // LANES, LANES), jnp.uint32),
                   jax.ShapeDtypeStruct((n, LANES), jnp.int32),
                   jax.ShapeDtypeStruct((n, LANES), jnp.float32)),
        grid=(n // tm,),
        in_specs=[pl.BlockSpec((tm, d), lambda i: (i, 0)),
                  pl.BlockSpec((1, d), lambda i: (0, 0)),
                  pl.BlockSpec((d, LANES), lambda i: (0, 0))],
        out_specs=(pl.BlockSpec((tm, d // 2 // LANES, LANES), lambda i: (i, 0, 0)),
                   pl.BlockSpec((tm, LANES), lambda i: (i, 0)),
                   pl.BlockSpec((tm, LANES), lambda i: (i, 0))),
        compiler_params=_cparams(1, 6 * tm * d * 4 + (8 << 20)),
        name="norm_route",
    )(x, gain.reshape(1, d), w_route)


_DMA_ISSUE_UNROLL = 8


def _fetch_step_indices(idx_hbm, idx_smem, sem_idx):
    b = pl.program_id(0)
    slot = b % 2

    def idx_copy(step, s):
        return pltpu.make_async_copy(idx_hbm.at[step], idx_smem.at[s], sem_idx.at[s])

    @pl.when(b == 0)
    def _():
        idx_copy(0, 0).start()

    idx_copy(b, slot).wait()

    @pl.when(b + 1 < pl.num_programs(0))
    def _():
        idx_copy(b + 1, 1 - slot).start()

    return slot


def _gather_rows(src_hbm, dst_ref, idx_smem, slot, sem_rows, n_rows):
    def start(r, carry):
        pltpu.make_async_copy(src_hbm.at[idx_smem[slot, 0, r]], dst_ref.at[r], sem_rows).start()
        return carry

    lax.fori_loop(0, n_rows, start, 0, unroll=_DMA_ISSUE_UNROLL)
    pltpu.make_async_copy(src_hbm.at[pl.ds(0, n_rows)], dst_ref, sem_rows).wait()


def _row_gather_kernel(idx_hbm, src_hbm, o_ref, idx_smem, sem_idx, sem_rows, *, rows):
    slot = _fetch_step_indices(idx_hbm, idx_smem, sem_idx)
    _gather_rows(src_hbm, o_ref, idx_smem, slot, sem_rows, rows)


def _row_gather(src, idx, rows):
    n_out = idx.shape[0]
    slab = src.shape[1:]
    d = slab[0] * slab[1]
    nb = n_out // rows
    return pl.pallas_call(
        functools.partial(_row_gather_kernel, rows=rows),
        out_shape=jax.ShapeDtypeStruct((n_out,) + slab, src.dtype),
        grid=(nb,),
        in_specs=[pl.BlockSpec(memory_space=pl.ANY), pl.BlockSpec(memory_space=pl.ANY)],
        out_specs=pl.BlockSpec((rows,) + slab, lambda b: (b, 0, 0)),
        scratch_shapes=[pltpu.SMEM((2, 1, rows), jnp.int32),
                        pltpu.SemaphoreType.DMA((2,)), pltpu.SemaphoreType.DMA],
        compiler_params=_cparams(1, 4 * rows * d * 4 + (4 << 20)),
        name="row_gather",
    )(idx.reshape(nb, 1, rows), src)


def _moe_up_kernel(be_ref, nb_ref, x_ref, w_ref, h_ref, *, f):
    b = pl.program_id(0)

    @pl.when(b < nb_ref[0])
    def _():
        x_hi, x_lo = _unpack_bf16_pairs(_load_row_slabs(x_ref, 0, x_ref.shape[0]))
        half = x_hi.shape[1]
        gu = (jnp.dot(x_hi.astype(jnp.bfloat16), w_ref[:half, :], preferred_element_type=jnp.float32)
              + jnp.dot(x_lo.astype(jnp.bfloat16), w_ref[half:, :], preferred_element_type=jnp.float32))
        g, u = gu[:, :f], gu[:, f:]
        h_ref[...] = (g * (1.0 / (1.0 + jnp.exp(-g))) * u).astype(h_ref.dtype)

    @pl.when(b >= nb_ref[0])
    def _():
        h_ref[...] = jnp.zeros(h_ref.shape, h_ref.dtype)


def _moe_down_kernel(be_ref, nb_ref, h_ref, sw_ref, w_ref, y_ref):
    b = pl.program_id(0)

    @pl.when(b < nb_ref[0])
    def _():
        y = jnp.dot(h_ref[...], w_ref[...], preferred_element_type=jnp.float32)
        _store_row_slabs(y_ref, _pack_bf16_pairs(y * sw_ref[...]))

    @pl.when(b >= nb_ref[0])
    def _():
        y_ref[...] = jnp.zeros(y_ref.shape, y_ref.dtype)


def _moe_experts(xs, slot_w, block_exp, n_blocks_used, w_gate_up, w_down, layer):
    n_slots = xs.shape[0]
    slab = xs.shape[1:]
    dh = slab[0] * slab[1]
    d = 2 * dh
    f = w_down.shape[2]
    nb = n_slots // MOE_BLOCK
    hmid = pl.pallas_call(
        functools.partial(_moe_up_kernel, f=f),
        out_shape=jax.ShapeDtypeStruct((n_slots, f), jnp.bfloat16),
        grid_spec=pltpu.PrefetchScalarGridSpec(
            num_scalar_prefetch=2, grid=(nb,),
            in_specs=[pl.BlockSpec((MOE_BLOCK,) + slab, lambda b, be, nu: (b, 0, 0)),
                      pl.BlockSpec((None, None, d, 2 * f), lambda b, be, nu: (layer, be[b], 0, 0))],
            out_specs=pl.BlockSpec((MOE_BLOCK, f), lambda b, be, nu: (b, 0))),
        compiler_params=_cparams(1, 2 * (MOE_BLOCK * d * 4 + d * 2 * f * 2) + 3 * MOE_BLOCK * 2 * f * 4 + (4 << 20)),
        name="moe_up",
    )(block_exp, n_blocks_used, xs, w_gate_up)
    return pl.pallas_call(
        _moe_down_kernel,
        out_shape=jax.ShapeDtypeStruct((n_slots,) + slab, jnp.uint32),
        grid_spec=pltpu.PrefetchScalarGridSpec(
            num_scalar_prefetch=2, grid=(nb,),
            in_specs=[pl.BlockSpec((MOE_BLOCK, f), lambda b, be, nu: (b, 0)),
                      pl.BlockSpec((MOE_BLOCK, 1), lambda b, be, nu: (b, 0)),
                      pl.BlockSpec((None, None, f, d), lambda b, be, nu: (layer, be[b], 0, 0))],
            out_specs=pl.BlockSpec((MOE_BLOCK,) + slab, lambda b, be, nu: (b, 0, 0))),
        compiler_params=_cparams(1, 2 * (f * d * 2 + MOE_BLOCK * d * 4) + 2 * MOE_BLOCK * d * 4 + (4 << 20)),
        name="moe_down",
    )(block_exp, n_blocks_used, hmid, slot_w.reshape(n_slots, 1), w_down)


def _moe_combine_kernel(pos_hbm, ys_hbm, x_ref, *refs, tm, nb0):
    *o_refs, idx_smem, ybuf, sem_idx, sem_rows = refs
    slot = _fetch_step_indices(pos_hbm, idx_smem, sem_idx)
    _gather_rows(ys_hbm, ybuf, idx_smem, slot, sem_rows, 2 * tm)
    a_hi, a_lo = _unpack_bf16_pairs(_load_row_slabs(ybuf, 0, tm))
    b_hi, b_lo = _unpack_bf16_pairs(_load_row_slabs(ybuf, tm, tm))
    half = a_hi.shape[1]
    out_hi = x_ref[:, :half] + (a_hi + b_hi)
    out_lo = x_ref[:, half:] + (a_lo + b_lo)
    if len(o_refs) == 1:
        o_refs[0][:, :half] = out_hi
        o_refs[0][:, half:] = out_lo
    else:
        for o_ref, mine in ((o_refs[0], pl.program_id(0) < nb0), (o_refs[1], pl.program_id(0) >= nb0)):
            @pl.when(mine)
            def _(o_ref=o_ref):
                o_ref[:, :half] = out_hi
                o_ref[:, half:] = out_lo


def _moe_combine(x, ys, pos, split_rows=None):
    n, d = x.shape
    tm = _tile(n if split_rows is None else math.gcd(split_rows, n - split_rows), MOE_BLOCK)
    nb = n // tm
    pos_tiles = pos.reshape(nb, tm, 2).transpose(0, 2, 1).reshape(nb, 1, 2 * tm)
    if split_rows is None:
        nb0 = nb
        out_shape = jax.ShapeDtypeStruct((n, d), x.dtype)
        out_specs = pl.BlockSpec((tm, d), lambda i: (i, 0))
    else:
        nb0 = split_rows // tm
        out_shape = (jax.ShapeDtypeStruct((split_rows, d), x.dtype), jax.ShapeDtypeStruct((n - split_rows, d), x.dtype))
        out_specs = (pl.BlockSpec((tm, d), lambda i: (jnp.minimum(i, nb0 - 1), 0)),
                     pl.BlockSpec((tm, d), lambda i: (jnp.maximum(i - nb0, 0), 0)))
    return pl.pallas_call(
        functools.partial(_moe_combine_kernel, tm=tm, nb0=nb0),
        out_shape=out_shape,
        grid=(nb,),
        in_specs=[pl.BlockSpec(memory_space=pl.ANY), pl.BlockSpec(memory_space=pl.ANY),
                  pl.BlockSpec((tm, d), lambda i: (i, 0))],
        out_specs=out_specs,
        scratch_shapes=[pltpu.SMEM((2, 1, 2 * tm), jnp.int32),
                        pltpu.VMEM((2 * tm,) + ys.shape[1:], jnp.uint32),
                        pltpu.SemaphoreType.DMA((2,)), pltpu.SemaphoreType.DMA],
        compiler_params=_cparams(1, 8 * tm * d * 4 + (4 << 20)),
        name="moe_combine",
    )(pos_tiles, ys, x)


def _moe(x, gain, w_route, w_gate_up, w_down, layer, split_rows=None):
    n, d = x.shape
    n_exp = N_EXPERT_GROUPS * EXPERTS_PER_GROUP
    h, eid, ew = _norm_route(x, gain, w_route)
    flat_e = eid[:, :2].reshape(-1)
    flat_w = ew[:, :2].reshape(-1)
    nk = 2 * n
    onehot = (flat_e[:, None] == jnp.arange(n_exp, dtype=jnp.int32)[None, :]).astype(jnp.int32)
    csum = jnp.cumsum(onehot, axis=0)
    rank = jnp.take_along_axis(csum, flat_e[:, None], axis=1)[:, 0] - 1
    counts = csum[-1]
    pcounts = (counts + MOE_BLOCK - 1) // MOE_BLOCK * MOE_BLOCK
    pend = jnp.cumsum(pcounts)
    pstart = pend - pcounts
    dest = pstart[flat_e] + rank
    n_blocks = -(-nk // MOE_BLOCK) + n_exp
    n_slots = n_blocks * MOE_BLOCK
    flat_tok = jnp.arange(nk, dtype=jnp.int32) // 2
    slot_tab = jnp.zeros((n_slots, 2), jnp.int32).at[dest].set(
        jnp.stack([flat_tok, lax.bitcast_convert_type(flat_w, jnp.int32)], axis=1))
    slot_tok = slot_tab[:, 0]
    slot_w = lax.bitcast_convert_type(slot_tab[:, 1], jnp.float32)
    block_exp = jnp.minimum(jnp.searchsorted(pend, jnp.arange(n_blocks, dtype=jnp.int32) * MOE_BLOCK, side='right'),
                            n_exp - 1).astype(jnp.int32)
    n_used = (pend[-1] // MOE_BLOCK).astype(jnp.int32).reshape(1)
    xs = _row_gather(h, slot_tok, MOE_BLOCK)
    ys = _moe_experts(xs, slot_w, block_exp, n_used, w_gate_up, w_down, layer)
    return _moe_combine(x, ys, dest.reshape(n, 2), split_rows)


def _alibi_slopes(n):
    return [2.0 ** (-8.0 * (i + 1) / n) for i in range(n)]


def _token_mixers(x, l, p, n_seq):
    n, d = sum(a.shape[0] for a in _segments(x)), _segments(x)[0].shape[1]
    s_len = n // n_seq
    bf16 = jnp.bfloat16
    n_ga = len(DIL_GROUPS)
    a_w = A_HEADS * HEAD_DIM
    a_cols = n_ga * a_w
    b_qk = B_HEADS * 2 * B_QK_DIM
    b_w = B_HEADS * 2 * B_QK_DIM
    in_cols = p['w_in'].shape[2]
    slopes = _alibi_slopes(n_ga * A_HEADS + B_HEADS)

    h = _rmsnorm(x, p['norm_mix'][l])
    proj = _matmul(h, p['w_in'], l, bf16, name="in_proj")
    gates = _matmul(h, p['w_gate'], l, bf16, sigmoid=True, name="gate_proj")
    proj3 = proj.reshape(n_seq, s_len, in_cols)

    outs, lses = [], []
    for g, (_, dilation) in enumerate(DIL_GROUPS):
        o, lse = _attn_a_group(proj3, p['qnorm_a'][l], p['knorm_a'][l], group=g, dilation=dilation,
                               slopes=tuple(slopes[g * A_HEADS:(g + 1) * A_HEADS]), in_cols=in_cols,
                               n_heads=A_HEADS)
        outs.append(o.reshape(n, a_w))
        lses.append(lse.reshape(n, LANES))
    oa = _combine_a(outs, lses, A_HEADS)

    lam_init = 0.8 - 0.6 * math.exp(-0.3 * l)
    lam_vecs = jnp.stack([p['lambda_q1'][l], p['lambda_k1'][l], p['lambda_q2'][l], p['lambda_k2'][l]])
    ob = _attn_b(proj3, p['qnorm_b'][l], p['knorm_b'][l], lam_vecs, p['subln_b'][l],
                 slopes[n_ga * A_HEADS:], lam_init=lam_init,
                 q_col=3 * a_cols, k_col=3 * a_cols + b_qk, v_col=3 * a_cols + 2 * b_qk,
                 n_heads=B_HEADS).reshape(n, b_w)

    c0 = 3 * a_cols + 2 * b_qk + b_w
    oc = _conv(proj3, p['conv_w'][l], u_col=c0, b_col=c0 + C_WIDTH, c_col=c0 + 2 * C_WIDTH).reshape(n, C_WIDTH)

    merged = _gated_proj(oa, ob, oc, gates, p['w_proj_a'], p['w_proj_b'], p['w_proj_c'], l)
    return _matmul(merged, p['w_out'], l, jnp.float32, residual=x, tn=512, name="out_proj")


def _route_weights(w_group, w_expert):
    d = w_group.shape[0]
    used = w_group.shape[1] + w_expert.shape[1]
    return jnp.concatenate([w_group, w_expert, jnp.zeros((d, LANES - used), w_group.dtype)],
                           axis=1).astype(jnp.bfloat16)


def kernel(x_prompt, x_sample, norm_mix, w_in, qnorm_a, knorm_a, qnorm_b, knorm_b, lambda_q1, lambda_k1,
           lambda_q2, lambda_k2, subln_b, conv_w, w_proj_a, w_proj_b, w_proj_c, w_gate, w_out, norm_ffn,
           w_route_group, w_route_expert, w_gate_up, w_down):
    bf16 = jnp.bfloat16
    p = dict(norm_mix=norm_mix, w_in=w_in.astype(bf16), qnorm_a=qnorm_a, knorm_a=knorm_a, qnorm_b=qnorm_b,
             knorm_b=knorm_b, lambda_q1=lambda_q1, lambda_k1=lambda_k1, lambda_q2=lambda_q2, lambda_k2=lambda_k2,
             subln_b=subln_b, conv_w=conv_w, w_proj_a=w_proj_a.astype(bf16), w_proj_b=w_proj_b.astype(bf16),
             w_proj_c=w_proj_c.astype(bf16), w_gate=w_gate.astype(bf16), w_out=w_out.astype(bf16))
    w_gate_up_b, w_down_b = w_gate_up.astype(bf16), w_down.astype(bf16)
    bp, s_len, d = x_prompt.shape
    bs = x_sample.shape[0]
    assert x_sample.shape[1:] == (s_len, d)
    n_seq = bp + bs
    x = (x_prompt.reshape(bp * s_len, d), x_sample.reshape(bs * s_len, d))
    depth = norm_mix.shape[0]
    for l in range(depth):
        x = _token_mixers(x, l, p, n_seq)
        x = _moe(x, norm_ffn[l], _route_weights(w_route_group[l], w_route_expert[l]), w_gate_up_b, w_down_b, l,
                 split_rows=bp * s_len if l == depth - 1 else None)
    return (x[0].reshape(bp, s_len, d), x[1].reshape(bs, s_len, d))
```

```python
import functools
import math

import jax
import jax.numpy as jnp
from jax import lax
from jax.experimental import pallas as pl
from jax.experimental.pallas import tpu as pltpu

HEAD_DIM = 128
DIL_GROUPS = ((128, 1), (512, 4), (2048, 16))
A_HEADS = 8
B_HEADS = 8
B_QK_DIM = 64
C_WIDTH = 2048
N_EXPERT_GROUPS = 4
EXPERTS_PER_GROUP = 8
D_FF_EXPERT = 1024
MOE_BLOCK = 256
RMS_EPS = 1e-6
NEG_BIG = -1e30

LANES = 128
V7X_VMEM_LIMIT_BYTES = 56 * 1024 * 1024

A_HALF = DIL_GROUPS[0][0] // (2 * DIL_GROUPS[0][1])


def _cparams(n_grid_dims, vmem_bytes):
    return pltpu.CompilerParams(
        dimension_semantics=("arbitrary",) * n_grid_dims,
        vmem_limit_bytes=int(min(max(vmem_bytes, 16 * 1024 * 1024), V7X_VMEM_LIMIT_BYTES)),
    )


def _tile(n, pref):
    t = min(n, pref)
    while n % t:
        t //= 2
    return t


def _segments(x):
    return tuple(x) if isinstance(x, (tuple, list)) else (x,)


def _segment_specs(segs, tm, tn, row_axis, col_of):
    nb0 = segs[0].shape[0] // tm
    if len(segs) == 1:
        return [pl.BlockSpec((tm, tn), lambda *g: (g[row_axis], col_of(*g)))], nb0
    return [pl.BlockSpec((tm, tn), lambda *g: (jnp.minimum(g[row_axis], nb0 - 1), col_of(*g))),
            pl.BlockSpec((tm, tn), lambda *g: (jnp.maximum(g[row_axis] - nb0, 0), col_of(*g)))], nb0


def _pick_segment(refs, step, nb0):
    if len(refs) == 1:
        return refs[0][...]
    return jnp.where(step < nb0, refs[0][...], refs[1][...])


def _rmsnorm_kernel(*refs, nb0):
    *x_refs, g_ref, o_ref = refs
    x = _pick_segment(x_refs, pl.program_id(0), nb0)
    ms = jnp.mean(x * x, axis=-1, keepdims=True)
    o_ref[...] = (x * lax.rsqrt(ms + RMS_EPS) * g_ref[...]).astype(o_ref.dtype)


def _rmsnorm(x, gain):
    segs = _segments(x)
    n, d = sum(a.shape[0] for a in segs), segs[0].shape[1]
    tm = _tile(min(a.shape[0] for a in segs), 256)
    x_specs, nb0 = _segment_specs(segs, tm, d, 0, lambda i: 0)
    return pl.pallas_call(
        functools.partial(_rmsnorm_kernel, nb0=nb0),
        out_shape=jax.ShapeDtypeStruct((n, d), jnp.bfloat16),
        grid=(n // tm,),
        in_specs=x_specs + [pl.BlockSpec((1, d), lambda i: (0, 0))],
        out_specs=pl.BlockSpec((tm, d), lambda i: (i, 0)),
        compiler_params=_cparams(1, (2 * len(segs) + 2) * tm * d * 4),
        name="rmsnorm",
    )(*segs, gain.reshape(1, d))


def _matmul_kernel(x_ref, w_ref, *refs, sigmoid, res_nb0):
    *r_refs, o_ref = refs
    acc = jnp.dot(x_ref[...], w_ref[...], preferred_element_type=jnp.float32)
    if sigmoid:
        acc = 1.0 / (1.0 + jnp.exp(-acc))
    if r_refs:
        acc = acc + _pick_segment(r_refs, pl.program_id(1), res_nb0)
    o_ref[...] = acc.astype(o_ref.dtype)


def _matmul(x, w, layer, out_dtype, *, sigmoid=False, residual=None, tm=1024, tn=1024, name="matmul"):
    n, k = x.shape
    c = w.shape[2]
    tm, tn = _tile(n, tm), _tile(c, tn)
    in_specs = [pl.BlockSpec((tm, k), lambda j, i: (i, 0)),
                pl.BlockSpec((None, k, tn), lambda j, i: (layer, 0, j))]
    args = [x, w]
    out_bytes = jnp.dtype(out_dtype).itemsize
    vmem = 2 * (tm * k * 2 + k * tn * 2 + tm * tn * out_bytes) + 2 * tm * tn * 4
    res_nb0 = 0
    if residual is not None:
        segs = _segments(residual)
        r_specs, res_nb0 = _segment_specs(segs, tm, tn, 1, lambda j, i: j)
        in_specs += r_specs
        args += list(segs)
        vmem += 2 * len(segs) * tm * tn * 4
    return pl.pallas_call(
        functools.partial(_matmul_kernel, sigmoid=sigmoid, res_nb0=res_nb0),
        out_shape=jax.ShapeDtypeStruct((n, c), out_dtype),
        grid=(c // tn, n // tm),
        in_specs=in_specs,
        out_specs=pl.BlockSpec((tm, tn), lambda j, i: (i, j)),
        compiler_params=_cparams(2, vmem + (4 << 20)),
        name=name,
    )(*args)


def _head_rmsnorm(x, gain):
    ms = jnp.mean(x * x, axis=-1, keepdims=True)
    return x * lax.rsqrt(ms + RMS_EPS) * gain


def _attn_a_kernel(q_ref, kp_ref, km_ref, kn_ref, vp_ref, vm_ref, vn_ref, gq_ref, gk_ref,
                   o_ref, lse_ref, qn_scr, kn_scr, v_scr, *, dilation, slopes, tu, sub):
    half = A_HALF
    i = pl.program_id(2)
    n_i = pl.num_programs(2)
    n_heads = len(slopes)
    scale = HEAD_DIM ** -0.5

    for h in range(n_heads):
        cs = slice(h * HEAD_DIM, (h + 1) * HEAD_DIM)
        qn_scr[:, cs] = _head_rmsnorm(q_ref[:, cs].astype(jnp.float32), gq_ref[...]).astype(qn_scr.dtype)
        for off, ref, rows in ((0, kp_ref, half), (half, km_ref, tu), (half + tu, kn_ref, half)):
            kn_scr[off:off + rows, cs] = _head_rmsnorm(ref[:, cs].astype(jnp.float32),
                                                       gk_ref[...]).astype(kn_scr.dtype)
    v_scr[0:half, :] = vp_ref[...]
    v_scr[half:half + tu, :] = vm_ref[...]
    v_scr[half + tu:, :] = vn_ref[...]

    kw = sub + 2 * half
    ii = lax.broadcasted_iota(jnp.int32, (sub, kw), 0)
    jj = lax.broadcasted_iota(jnp.int32, (sub, kw), 1)
    rel = jnp.abs(jj - half - ii)
    band = rel <= half
    dist = (rel * dilation).astype(jnp.float32)
    lane = lax.broadcasted_iota(jnp.int32, (sub, LANES), 1)

    def body(t, carry):
        a = pl.multiple_of(t * sub, sub)
        kpos = a + jj
        valid = band & ((kpos >= half) | (i > 0)) & ((kpos < tu + half) | (i < n_i - 1))
        lse_tile = jnp.zeros((sub, LANES), jnp.float32)
        for h in range(n_heads):
            cs = slice(h * HEAD_DIM, (h + 1) * HEAD_DIM)
            qh = qn_scr[pl.ds(a, sub), cs]
            kh = kn_scr[pl.ds(a, kw), cs]
            vh = v_scr[pl.ds(a, kw), cs]
            s = lax.dot_general(qh, kh, (((1,), (1,)), ((), ())), preferred_element_type=jnp.float32)
            s = jnp.where(valid, s * scale - slopes[h] * dist, NEG_BIG)
            m = jnp.max(s, axis=-1, keepdims=True)
            p = jnp.exp(s - m)
            l = jnp.sum(p, axis=-1, keepdims=True)
            o = jnp.dot(p.astype(vh.dtype), vh, preferred_element_type=jnp.float32) / l
            o_ref[pl.ds(a, sub), cs] = o.astype(o_ref.dtype)
            lse_tile = jnp.where(lane == h, m + jnp.log(l), lse_tile)
        lse_ref[pl.ds(a, sub), :] = lse_tile
        return carry

    lax.fori_loop(0, tu // sub, body, 0)


def _deinterleave_kernel(*refs, dilation, width):
    *in_refs, o_ref, scr = refs
    per = in_refs[0].shape[0] // dilation
    n_lane_blocks = width // LANES
    for c, ref in enumerate(in_refs):
        for cc in range(n_lane_blocks):
            scr[cc] = ref[:, cc * LANES:(cc + 1) * LANES].astype(jnp.float32)
        for r in range(dilation):
            col = (r * len(in_refs) + c) * width
            for cc in range(n_lane_blocks):
                o_ref[:, col + cc * LANES:col + (cc + 1) * LANES] = (
                    scr[cc, pl.ds(r, per, stride=dilation), :].astype(o_ref.dtype))


def _deinterleave(proj3, col_blocks, width, dilation):
    n_seq, s_len, _ = proj3.shape
    rt = _tile(s_len, 1024)
    nc = len(col_blocks)
    return pl.pallas_call(
        functools.partial(_deinterleave_kernel, dilation=dilation, width=width),
        out_shape=jax.ShapeDtypeStruct((n_seq, s_len // dilation, dilation * nc * width), proj3.dtype),
        grid=(n_seq, s_len // rt),
        in_specs=[pl.BlockSpec((None, rt, width), lambda b, t, cb=cb: (b, t, cb)) for cb in col_blocks],
        out_specs=pl.BlockSpec((None, rt // dilation, dilation * nc * width), lambda b, t: (b, t, 0)),
        scratch_shapes=[pltpu.VMEM((width // LANES, rt, LANES), jnp.float32)],
        compiler_params=_cparams(2, V7X_VMEM_LIMIT_BYTES),
        name=f"deinterleave_d{dilation}",
    )(*([proj3] * nc))


def _attn_a_group(proj3, gq, gk, *, group, dilation, slopes, in_cols, n_heads):
    n_seq, s_len, _ = proj3.shape
    half = A_HALF
    u_len = s_len // dilation
    width = n_heads * HEAD_DIM
    a_cols = len(DIL_GROUPS) * width
    tu = _tile(u_len, 512)
    sub = min(tu, 128)
    q_blk = group
    k_blk = a_cols // width + group
    v_blk = 2 * a_cols // width + group
    if dilation > 1:
        pv = _deinterleave(proj3, (q_blk, k_blk, v_blk), width, dilation)
        in_cols, q_blk, k_blk, v_blk = 3 * width, 0, 1, 2
    else:
        pv = proj3
    blocks_per_row = in_cols // width
    nh = tu // half
    n_halo = u_len // half

    def main(col):
        return pl.BlockSpec((None, tu, width), lambda b, r, i: (b, i, r * blocks_per_row + col))

    def prev(col):
        return pl.BlockSpec((None, half, width),
                            lambda b, r, i: (b, jnp.maximum(i * nh - 1, 0), r * blocks_per_row + col))

    def nxt(col):
        return pl.BlockSpec((None, half, width),
                            lambda b, r, i: (b, jnp.minimum((i + 1) * nh, n_halo - 1), r * blocks_per_row + col))

    gspec = pl.BlockSpec((1, HEAD_DIM), lambda b, r, i: (0, 0))
    o, lse = pl.pallas_call(
        functools.partial(_attn_a_kernel, dilation=dilation, slopes=slopes, tu=tu, sub=sub),
        out_shape=(jax.ShapeDtypeStruct((n_seq, u_len, dilation * width), jnp.bfloat16),
                   jax.ShapeDtypeStruct((n_seq, u_len, dilation * LANES), jnp.float32)),
        grid=(n_seq, dilation, u_len // tu),
        in_specs=[main(q_blk), prev(k_blk), main(k_blk), nxt(k_blk),
                  prev(v_blk), main(v_blk), nxt(v_blk), gspec, gspec],
        out_specs=(pl.BlockSpec((None, tu, width), lambda b, r, i: (b, i, r)),
                   pl.BlockSpec((None, tu, LANES), lambda b, r, i: (b, i, r))),
        scratch_shapes=[pltpu.VMEM((tu, width), jnp.bfloat16),
                        pltpu.VMEM((tu + 2 * half, width), jnp.bfloat16),
                        pltpu.VMEM((tu + 2 * half, width), jnp.bfloat16)],
        compiler_params=_cparams(3, 32 << 20),
        name=f"attn_a_g{group}",
    )(pv, pv, pv, pv, pv, pv, pv, gq.reshape(1, HEAD_DIM), gk.reshape(1, HEAD_DIM))
    return o.reshape(n_seq, s_len, width), lse.reshape(n_seq, s_len, LANES)


def _combine_a_kernel(o0_ref, o1_ref, o2_ref, l0_ref, l1_ref, l2_ref, out_ref, *, n_heads):
    l0, l1, l2 = l0_ref[...], l1_ref[...], l2_ref[...]
    m = jnp.maximum(jnp.maximum(l0, l1), l2)
    e0, e1, e2 = jnp.exp(l0 - m), jnp.exp(l1 - m), jnp.exp(l2 - m)
    den = e0 + e1 + e2
    w0, w1, w2 = e0 / den, e1 / den, e2 / den
    for h in range(n_heads):
        cs = slice(h * HEAD_DIM, (h + 1) * HEAD_DIM)
        acc = (w0[:, h:h + 1] * o0_ref[:, cs].astype(jnp.float32)
               + w1[:, h:h + 1] * o1_ref[:, cs].astype(jnp.float32)
               + w2[:, h:h + 1] * o2_ref[:, cs].astype(jnp.float32))
        out_ref[:, cs] = acc.astype(out_ref.dtype)


def _combine_a(outs, lses, n_heads):
    n, width = outs[0].shape
    tm = _tile(n, 512)
    ospec = pl.BlockSpec((tm, width), lambda i: (i, 0))
    lspec = pl.BlockSpec((tm, LANES), lambda i: (i, 0))
    return pl.pallas_call(
        functools.partial(_combine_a_kernel, n_heads=n_heads),
        out_shape=jax.ShapeDtypeStruct((n, width), jnp.bfloat16),
        grid=(n // tm,),
        in_specs=[ospec, ospec, ospec, lspec, lspec, lspec],
        out_specs=ospec,
        compiler_params=_cparams(1, 24 << 20),
        name="combine_a",
    )(*outs, *lses)


_B_FEAT = 4
_B_PAD_ROWS = 16


def _split_hi_lo(x):
    hi = x.astype(jnp.bfloat16)
    lo = (x - hi.astype(jnp.float32)).astype(jnp.bfloat16)
    return hi.astype(jnp.float32), lo.astype(jnp.float32)


_LOG2E = 1.4426950408889634


def _attn_b_kernel(q_ref, k_ref, v_ref, gq_ref, gk_ref, lam_ref, gs_ref, slope_ref, o_ref,
                   k_scr, vt_scr, q_scr, s_scr, m_scr, acc_scr, *, lam_init, tq, tk, s_len):
    i = pl.program_id(2)
    dq = B_QK_DIM
    n_kt = s_len // tk
    slope = slope_ref[...] * _LOG2E
    slope1 = slope[:, 0:1]

    lane_q = lax.broadcasted_iota(jnp.int32, (tq, 2 * dq), 1)
    lane_k = lax.broadcasted_iota(jnp.int32, (tk, 2 * dq), 1)

    def norm_maps(x, gain, lane):
        in0 = lane < dq
        sq = x * x
        ms0 = jnp.sum(jnp.where(in0, sq, 0.0), axis=-1, keepdims=True) / dq
        ms1 = jnp.sum(jnp.where(in0, 0.0, sq), axis=-1, keepdims=True) / dq
        inv = jnp.where(in0, lax.rsqrt(ms0 + RMS_EPS), lax.rsqrt(ms1 + RMS_EPS))
        return x * inv * gain

    @pl.when(i == 0)
    def _():
        def kbody(t, carry):
            r0 = pl.multiple_of(t * tk, tk)
            kn = norm_maps(k_ref[pl.ds(r0, tk), :].astype(jnp.float32), gk_ref[...], lane_k)
            b = lax.broadcasted_iota(jnp.int32, (tk, 2 * dq), 0).astype(jnp.float32)
            l_hi, l_lo = _split_hi_lo(slope * b)
            r_hi, r_lo = _split_hi_lo(slope * (tk - 1 - b))
            for m in range(2):
                f0 = (1 - m) * dq
                feat = jnp.where(lane_k == f0, l_hi,
                       jnp.where(lane_k == f0 + 1, l_lo,
                       jnp.where(lane_k == f0 + 2, r_hi,
                       jnp.where(lane_k == f0 + 3, r_lo, 0.0))))
                own = (lane_k >= m * dq) & (lane_k < (m + 1) * dq)
                k_scr[m, pl.ds(r0, tk), :] = jnp.where(own, kn, feat).astype(k_scr.dtype)
            vt_scr[t, 0:2 * dq, :] = v_ref[pl.ds(r0, tk), :].astype(jnp.float32).T.astype(vt_scr.dtype)
            row = lax.broadcasted_iota(jnp.int32, (_B_PAD_ROWS, tk), 0)
            vt_scr[t, 2 * dq:, :] = jnp.where(row == 0, 1.0, 0.0).astype(vt_scr.dtype)
            return carry
        lax.fori_loop(0, n_kt, kbody, 0)

    qn = norm_maps(q_ref[...].astype(jnp.float32), gq_ref[...], lane_q) * (dq ** -0.5 * _LOG2E)
    for m in range(2):
        f0 = (1 - m) * dq
        own = (lane_q >= m * dq) & (lane_q < (m + 1) * dq)
        left = (lane_q == f0) | (lane_q == f0 + 1)
        right = (lane_q == f0 + 2) | (lane_q == f0 + 3)
        q_scr[m, 0] = jnp.where(own, qn, 0.0).astype(q_scr.dtype)
        q_scr[m, 1] = jnp.where(own, qn, jnp.where(left, 1.0, 0.0)).astype(q_scr.dtype)
        q_scr[m, 2] = jnp.where(own, qn, jnp.where(right, 1.0, 0.0)).astype(q_scr.dtype)

    m_scr[...] = jnp.full(m_scr.shape, NEG_BIG, jnp.float32)
    acc_scr[...] = jnp.zeros(acc_scr.shape, jnp.float32)

    q_pos = i * tq + lax.broadcasted_iota(jnp.int32, (1, tq), 1)

    i_kt = (i * tq) // tk

    def scores(j, s_ref):
        j = jnp.asarray(j, jnp.int32)
        r0 = pl.multiple_of(j * tk, tk)
        ver = jnp.where(j == i_kt, 0, jnp.where(j < i_kt, 1, 2))
        for m in range(2):
            s_ref[m] = lax.dot_general(k_scr[m, pl.ds(r0, tk), :], q_scr[m, ver], (((1,), (1,)), ((), ())),
                                       preferred_element_type=jnp.float32)

    def diagonal_bias(j, s_ref):
        @pl.when(j == i_kt)
        def _():
            a = lax.broadcasted_iota(jnp.int32, (tk, tq), 0)
            b = lax.broadcasted_iota(jnp.int32, (tk, tq), 1)
            bias = slope1 * jnp.abs(a - b - (i * tq - i_kt * tk)).astype(jnp.float32)
            for m in range(2):
                s_ref[m] = s_ref[m] - bias

    def softmax_pv(j, s_ref):
        j = jnp.asarray(j, jnp.int32)
        sgn = jnp.where(j < i_kt, 1, jnp.where(j > i_kt, -1, 0))
        off = jnp.where(j > i_kt, tk - 1, 0)
        shift = slope1 * (sgn * (j * tk - q_pos) - off).astype(jnp.float32)
        vt = vt_scr[j]
        for m in range(2):
            s = s_ref[m]
            m_old = m_scr[m]
            m_new = jnp.maximum(m_old, jnp.max(s, axis=0, keepdims=True) + shift)
            p = jnp.exp2(s - (m_new - shift))
            alpha = jnp.exp2(m_old - m_new)
            acc_scr[m] = alpha * acc_scr[m] + jnp.dot(vt, p.astype(vt.dtype),
                                                      preferred_element_type=jnp.float32)
            m_scr[m] = m_new

    s_a, s_b = s_scr.at[0], s_scr.at[1]
    scores(0, s_a)
    diagonal_bias(0, s_a)

    def pair(t, carry):
        j = 2 * t
        scores(j + 1, s_b)
        softmax_pv(j, s_a)
        diagonal_bias(j + 1, s_b)
        scores(j + 2, s_a)
        softmax_pv(j + 1, s_b)
        diagonal_bias(j + 2, s_a)
        return carry

    lax.fori_loop(0, n_kt // 2 - 1, pair, 0)
    scores(n_kt - 1, s_b)
    softmax_pv(n_kt - 2, s_a)
    diagonal_bias(n_kt - 1, s_b)
    softmax_pv(n_kt - 1, s_b)

    lam_v = lam_ref[...]
    lam = (jnp.exp(jnp.sum(lam_v[0:1] * lam_v[1:2], axis=-1, keepdims=True))
           - jnp.exp(jnp.sum(lam_v[2:3] * lam_v[3:4], axis=-1, keepdims=True)) + lam_init)
    hw = 2 * dq
    out = (acc_scr[0, 0:hw, :] * (1.0 / acc_scr[0, hw:hw + 1, :])
           - lam * (acc_scr[1, 0:hw, :] * (1.0 / acc_scr[1, hw:hw + 1, :])))
    ms = jnp.mean(out * out, axis=0, keepdims=True)
    out = out * lax.rsqrt(ms + RMS_EPS) * gs_ref[...] * (1.0 - lam_init)
    o_ref[...] = out.T.astype(o_ref.dtype)


def _attn_b(proj3, gq, gk, lam_vecs, subln, slopes, *, lam_init, q_col, k_col, v_col, n_heads):
    n_seq, s_len, _ = proj3.shape
    hw = 2 * B_QK_DIM
    tq, tk = _tile(s_len, 1024), _tile(s_len, 1024)
    assert (s_len // tk) % 2 == 0 and tk % tq == 0, "key tiles are pipelined in pairs; a query tile sits in one key tile"
    qb, kb, vb = q_col // hw, k_col // hw, v_col // hw
    gq2 = jnp.tile(gq, 2).reshape(1, hw)
    gk2 = jnp.tile(gk, 2).reshape(1, hw)
    slope_arr = jnp.broadcast_to(jnp.asarray(slopes, jnp.float32)[:, None, None], (n_heads, 1, LANES))
    small = lambda shape: pl.BlockSpec(shape, lambda b, h, i: (0, 0))
    return pl.pallas_call(
        functools.partial(_attn_b_kernel, lam_init=lam_init, tq=tq, tk=tk, s_len=s_len),
        out_shape=jax.ShapeDtypeStruct((n_seq, s_len, n_heads * hw), jnp.bfloat16),
        grid=(n_seq, n_heads, s_len // tq),
        in_specs=[pl.BlockSpec((None, tq, hw), lambda b, h, i: (b, i, qb + h)),
                  pl.BlockSpec((None, s_len, hw), lambda b, h, i: (b, 0, kb + h)),
                  pl.BlockSpec((None, s_len, hw), lambda b, h, i: (b, 0, vb + h)),
                  small((1, hw)), small((1, hw)), small((4, B_QK_DIM)), small((hw, 1)),
                  pl.BlockSpec((None, 1, LANES), lambda b, h, i: (h, 0, 0))],
        out_specs=pl.BlockSpec((None, tq, hw), lambda b, h, i: (b, i, h)),
        scratch_shapes=[pltpu.VMEM((2, s_len, hw), jnp.bfloat16),
                        pltpu.VMEM((s_len // tk, hw + _B_PAD_ROWS, tk), jnp.bfloat16),
                        pltpu.VMEM((2, 3, tq, hw), jnp.bfloat16),
                        pltpu.VMEM((2, 2, tk, tq), jnp.float32),
                        pltpu.VMEM((2, 1, tq), jnp.float32),
                        pltpu.VMEM((2, hw + _B_PAD_ROWS, tq), jnp.float32)],
        compiler_params=_cparams(3, V7X_VMEM_LIMIT_BYTES),
        name="attn_b",
    )(proj3, proj3, proj3, gq2, gk2, lam_vecs, subln.reshape(hw, 1), slope_arr)


_CONV_HALO = 16


def _conv_kernel(u_ref, b_ref, c_ref, up_ref, cp_ref, un_ref, cn_ref, w_ref, o_ref, *, tm):
    i = pl.program_id(1)
    n_i = pl.num_programs(1)
    f32 = jnp.float32
    v = c_ref[...].astype(f32) * u_ref[...].astype(f32)
    v_prev = (cp_ref[_CONV_HALO - 1:_CONV_HALO, :].astype(f32) * up_ref[_CONV_HALO - 1:_CONV_HALO, :].astype(f32))
    v_next = cn_ref[0:1, :].astype(f32) * un_ref[0:1, :].astype(f32)
    v_prev = jnp.where(i > 0, v_prev, 0.0)
    v_next = jnp.where(i < n_i - 1, v_next, 0.0)
    row = lax.broadcasted_iota(jnp.int32, v.shape, 0)
    down = jnp.where(row == 0, v_prev, pltpu.roll(v, 1, axis=0))
    up = jnp.where(row == tm - 1, v_next, pltpu.roll(v, tm - 1, axis=0))
    w = w_ref[...]
    y = down * w[0:1] + v * w[1:2] + up * w[2:3]
    o_ref[...] = (b_ref[...].astype(f32) * y).astype(o_ref.dtype)


def _conv(proj3, conv_w, *, u_col, b_col, c_col):
    n_seq, s_len, _ = proj3.shape
    cw = conv_w.shape[1]
    tm = _tile(s_len, 512)
    tc = _tile(cw, 1024)
    ncb = cw // tc
    nh = tm // _CONV_HALO
    n_halo = s_len // _CONV_HALO

    def main(col):
        return pl.BlockSpec((None, tm, tc), lambda b, i, c: (b, i, col // tc + c))

    def prev(col):
        return pl.BlockSpec((None, _CONV_HALO, tc),
                            lambda b, i, c: (b, jnp.maximum(i * nh - 1, 0), col // tc + c))

    def nxt(col):
        return pl.BlockSpec((None, _CONV_HALO, tc),
                            lambda b, i, c: (b, jnp.minimum((i + 1) * nh, n_halo - 1), col // tc + c))

    return pl.pallas_call(
        functools.partial(_conv_kernel, tm=tm),
        out_shape=jax.ShapeDtypeStruct((n_seq, s_len, cw), jnp.bfloat16),
        grid=(n_seq, s_len // tm, ncb),
        in_specs=[main(u_col), main(b_col), main(c_col), prev(u_col), prev(c_col), nxt(u_col), nxt(c_col),
                  pl.BlockSpec((3, tc), lambda b, i, c: (0, c))],
        out_specs=pl.BlockSpec((None, tm, tc), lambda b, i, c: (b, i, c)),
        compiler_params=_cparams(3, 32 << 20),
        name="short_conv",
    )(proj3, proj3, proj3, proj3, proj3, proj3, proj3, conv_w)


def _gated_proj_kernel(oa_ref, ob_ref, oc_ref, g0_ref, g1_ref, g2_ref, wa_ref, wb_ref, wc_ref, o_ref):
    f32 = jnp.float32
    acc = g0_ref[...].astype(f32) * jnp.dot(oa_ref[...], wa_ref[...], preferred_element_type=f32)
    acc += g1_ref[...].astype(f32) * jnp.dot(ob_ref[...], wb_ref[...], preferred_element_type=f32)
    acc += g2_ref[...].astype(f32) * jnp.dot(oc_ref[...], wc_ref[...], preferred_element_type=f32)
    o_ref[...] = acc.astype(o_ref.dtype)


def _gated_proj(oa, ob, oc, gates, wa, wb, wc, layer):
    n, d = oa.shape[0], wa.shape[2]
    tm, tn = _tile(n, 512), _tile(d, 1024)
    nj = d // tn
    act = lambda a: pl.BlockSpec((tm, a.shape[1]), lambda j, i: (i, 0))
    gate = lambda br: pl.BlockSpec((tm, tn), lambda j, i: (i, br * nj + j))
    wsp = lambda w: pl.BlockSpec((None, w.shape[1], tn), lambda j, i: (layer, 0, j))
    ka, kb, kc = oa.shape[1], ob.shape[1], oc.shape[1]
    vmem = 2 * 2 * (tm * (ka + kb + kc) + 3 * tm * tn + (ka + kb + kc) * tn + tm * tn) + 4 * tm * tn * 4
    return pl.pallas_call(
        _gated_proj_kernel,
        out_shape=jax.ShapeDtypeStruct((n, d), jnp.bfloat16),
        grid=(nj, n // tm),
        in_specs=[act(oa), act(ob), act(oc), gate(0), gate(1), gate(2), wsp(wa), wsp(wb), wsp(wc)],
        out_specs=pl.BlockSpec((tm, tn), lambda j, i: (i, j)),
        compiler_params=_cparams(2, vmem + (4 << 20)),
        name="gated_proj",
    )(oa, ob, oc, gates, gates, gates, wa, wb, wc)


def _pack_bf16_pairs(x):
    c = x.shape[1] // 2
    bits = lax.bitcast_convert_type(x.astype(jnp.bfloat16).astype(jnp.float32), jnp.uint32)
    return (bits[:, :c] & jnp.uint32(0xFFFF0000)) | (bits[:, c:] >> 16)


def _unpack_bf16_pairs(p):
    hi = lax.bitcast_convert_type(p & jnp.uint32(0xFFFF0000), jnp.float32)
    lo = lax.bitcast_convert_type(p << 16, jnp.float32)
    return hi, lo


def _norm_route_kernel(x_ref, g_ref, wr_ref, h_ref, eid_ref, ew_ref):
    x = x_ref[...]
    ms = jnp.mean(x * x, axis=-1, keepdims=True)
    h = x * lax.rsqrt(ms + RMS_EPS) * g_ref[...]
    h_ref[...] = _pack_bf16_pairs(h)
    logits = jnp.dot(h.astype(jnp.bfloat16), wr_ref[...], preferred_element_type=jnp.float32)
    ng, ne = N_EXPERT_GROUPS, EXPERTS_PER_GROUP
    lane = lax.broadcasted_iota(jnp.int32, logits.shape, 1)
    big = jnp.int32(1 << 20)
    is_g = lane < ng
    gl = jnp.where(is_g, logits, NEG_BIG)
    gmax = jnp.max(gl, axis=-1, keepdims=True)
    garg = jnp.min(jnp.where(is_g & (gl == gmax), lane, big), axis=-1, keepdims=True)
    g_w = 1.0 / jnp.sum(jnp.where(is_g, jnp.exp(gl - gmax), 0.0), axis=-1, keepdims=True)
    lo = ng + ne * garg
    sel = (lane >= lo) & (lane < lo + ne)
    el = jnp.where(sel, logits, NEG_BIG)
    m1 = jnp.max(el, axis=-1, keepdims=True)
    a1 = jnp.min(jnp.where(sel & (el == m1), lane, big), axis=-1, keepdims=True)
    el2 = jnp.where(lane == a1, NEG_BIG, el)
    m2 = jnp.max(el2, axis=-1, keepdims=True)
    a2 = jnp.min(jnp.where(sel & (lane != a1) & (el2 == m2), lane, big), axis=-1, keepdims=True)
    t = jnp.exp(m2 - m1)
    w1 = g_w / (1.0 + t)
    w2 = g_w * t / (1.0 + t)
    eid_ref[...] = jnp.where(lane == 0, a1 - ng, jnp.where(lane == 1, a2 - ng, 0))
    ew_ref[...] = jnp.where(lane == 0, w1, jnp.where(lane == 1, w2, 0.0))


def _norm_route(x, gain, w_route):
    n, d = x.shape
    tm = _tile(n, 256)
    return pl.pallas_call(
        _norm_route_kernel,
        out_shape=(jax.ShapeDtypeStruct((n, d // 2), jnp.uint32),
                   jax.ShapeDtypeStruct((n, LANES), jnp.int32),
                   jax.ShapeDtypeStruct((n, LANES), jnp.float32)),
        grid=(n // tm,),
        in_specs=[pl.BlockSpec((tm, d), lambda i: (i, 0)),
                  pl.BlockSpec((1, d), lambda i: (0, 0)),
                  pl.BlockSpec((d, LANES), lambda i: (0, 0))],
        out_specs=(pl.BlockSpec((tm, d // 2), lambda i: (i, 0)),
                   pl.BlockSpec((tm, LANES), lambda i: (i, 0)),
                   pl.BlockSpec((tm, LANES), lambda i: (i, 0))),
        compiler_params=_cparams(1, 6 * tm * d * 4 + (8 << 20)),
        name="norm_route",
    )(x, gain.reshape(1, d), w_route)


_DMA_ISSUE_UNROLL = 8
_GATHER_ROWS = 4 * MOE_BLOCK


def _fetch_step_indices(idx_hbm, idx_smem, sem_idx):
    b = pl.program_id(0)
    slot = b % 2

    def idx_copy(step, s):
        return pltpu.make_async_copy(idx_hbm.at[step], idx_smem.at[s], sem_idx.at[s])

    @pl.when(b == 0)
    def _():
        idx_copy(0, 0).start()

    idx_copy(b, slot).wait()

    @pl.when(b + 1 < pl.num_programs(0))
    def _():
        idx_copy(b + 1, 1 - slot).start()

    return slot


def _gather_rows(src_hbm, dst_ref, idx_smem, slot, sem_rows, n_rows):
    def start_group(g, carry):
        for k in range(_DMA_ISSUE_UNROLL):
            r = g * _DMA_ISSUE_UNROLL + k
            pltpu.make_async_copy(src_hbm.at[pl.ds(idx_smem[slot, 0, r], 1), :], dst_ref.at[pl.ds(r, 1), :],
                                  sem_rows).start(priority=k % 2)
        return carry

    lax.fori_loop(0, n_rows // _DMA_ISSUE_UNROLL, start_group, 0)
    pltpu.make_async_copy(src_hbm.at[pl.ds(0, n_rows), :], dst_ref, sem_rows).wait()


def _row_gather_kernel(idx_hbm, src_hbm, o_ref, idx_smem, sem_idx, sem_rows, *, rows):
    slot = _fetch_step_indices(idx_hbm, idx_smem, sem_idx)
    _gather_rows(src_hbm, o_ref, idx_smem, slot, sem_rows, rows)


def _row_gather(src, idx, rows):
    n_out = idx.shape[0]
    d = src.shape[1]
    nb = n_out // rows
    return pl.pallas_call(
        functools.partial(_row_gather_kernel, rows=rows),
        out_shape=jax.ShapeDtypeStruct((n_out, d), src.dtype),
        grid=(nb,),
        in_specs=[pl.BlockSpec(memory_space=pl.ANY), pl.BlockSpec(memory_space=pl.ANY)],
        out_specs=pl.BlockSpec((rows, d), lambda b: (b, 0)),
        scratch_shapes=[pltpu.SMEM((2, 1, rows), jnp.int32),
                        pltpu.SemaphoreType.DMA((2,)), pltpu.SemaphoreType.DMA],
        compiler_params=_cparams(1, 4 * rows * d * 4 + (4 << 20)),
        name="row_gather",
    )(idx.reshape(nb, 1, rows), src)


def _moe_up_kernel(be_ref, nb_ref, x_ref, w_ref, h_ref, *, f):
    b = pl.program_id(0)

    @pl.when(b < nb_ref[0])
    def _():
        x_hi, x_lo = _unpack_bf16_pairs(x_ref[...])
        half = x_hi.shape[1]
        gu = (jnp.dot(x_hi.astype(jnp.bfloat16), w_ref[:half, :], preferred_element_type=jnp.float32)
              + jnp.dot(x_lo.astype(jnp.bfloat16), w_ref[half:, :], preferred_element_type=jnp.float32))
        g, u = gu[:, :f], gu[:, f:]
        h_ref[...] = (g * (1.0 / (1.0 + jnp.exp(-g))) * u).astype(h_ref.dtype)

    @pl.when(b >= nb_ref[0])
    def _():
        h_ref[...] = jnp.zeros(h_ref.shape, h_ref.dtype)


def _moe_down_kernel(be_ref, nb_ref, h_ref, sw_ref, w_ref, y_ref):
    b = pl.program_id(0)

    @pl.when(b < nb_ref[0])
    def _():
        y = jnp.dot(h_ref[...], w_ref[...], preferred_element_type=jnp.float32)
        y_ref[...] = _pack_bf16_pairs(y * sw_ref[...])

    @pl.when(b >= nb_ref[0])
    def _():
        y_ref[...] = jnp.zeros(y_ref.shape, y_ref.dtype)


def _moe_experts(xs, slot_w, block_exp, n_blocks_used, w_gate_up, w_down, layer):
    n_slots, dh = xs.shape
    d = 2 * dh
    f = w_down.shape[2]
    nb = n_slots // MOE_BLOCK
    hmid = pl.pallas_call(
        functools.partial(_moe_up_kernel, f=f),
        out_shape=jax.ShapeDtypeStruct((n_slots, f), jnp.bfloat16),
        grid_spec=pltpu.PrefetchScalarGridSpec(
            num_scalar_prefetch=2, grid=(nb,),
            in_specs=[pl.BlockSpec((MOE_BLOCK, dh), lambda b, be, nu: (b, 0)),
                      pl.BlockSpec((None, None, d, 2 * f), lambda b, be, nu: (layer, be[b], 0, 0))],
            out_specs=pl.BlockSpec((MOE_BLOCK, f), lambda b, be, nu: (b, 0))),
        compiler_params=_cparams(1, 2 * (MOE_BLOCK * d * 4 + d * 2 * f * 2) + 3 * MOE_BLOCK * 2 * f * 4 + (4 << 20)),
        name="moe_up",
    )(block_exp, n_blocks_used, xs, w_gate_up)
    return pl.pallas_call(
        _moe_down_kernel,
        out_shape=jax.ShapeDtypeStruct((n_slots, dh), jnp.uint32),
        grid_spec=pltpu.PrefetchScalarGridSpec(
            num_scalar_prefetch=2, grid=(nb,),
            in_specs=[pl.BlockSpec((MOE_BLOCK, f), lambda b, be, nu: (b, 0)),
                      pl.BlockSpec((MOE_BLOCK, 1), lambda b, be, nu: (b, 0)),
                      pl.BlockSpec((None, None, f, d), lambda b, be, nu: (layer, be[b], 0, 0))],
            out_specs=pl.BlockSpec((MOE_BLOCK, dh), lambda b, be, nu: (b, 0))),
        compiler_params=_cparams(1, 2 * (f * d * 2 + MOE_BLOCK * d * 4) + 2 * MOE_BLOCK * d * 4 + (4 << 20)),
        name="moe_down",
    )(block_exp, n_blocks_used, hmid, slot_w.reshape(n_slots, 1), w_down)


def _moe_combine_kernel(pos_hbm, ys_hbm, x_ref, *refs, tm, nb0):
    *o_refs, idx_smem, ybuf, sem_idx, sem_rows = refs
    slot = _fetch_step_indices(pos_hbm, idx_smem, sem_idx)
    _gather_rows(ys_hbm, ybuf, idx_smem, slot, sem_rows, 2 * tm)
    a_hi, a_lo = _unpack_bf16_pairs(ybuf[0:tm, :])
    b_hi, b_lo = _unpack_bf16_pairs(ybuf[tm:2 * tm, :])
    half = a_hi.shape[1]
    out_hi = x_ref[:, :half] + (a_hi + b_hi)
    out_lo = x_ref[:, half:] + (a_lo + b_lo)
    if len(o_refs) == 1:
        o_refs[0][:, :half] = out_hi
        o_refs[0][:, half:] = out_lo
    else:
        for o_ref, mine in ((o_refs[0], pl.program_id(0) < nb0), (o_refs[1], pl.program_id(0) >= nb0)):
            @pl.when(mine)
            def _(o_ref=o_ref):
                o_ref[:, :half] = out_hi
                o_ref[:, half:] = out_lo


def _moe_combine(x, ys, pos, split_rows=None):
    n, d = x.shape
    tm = _tile(n if split_rows is None else math.gcd(split_rows, n - split_rows), MOE_BLOCK)
    nb = n // tm
    pos_tiles = pos.reshape(nb, tm, 2).transpose(0, 2, 1).reshape(nb, 1, 2 * tm)
    if split_rows is None:
        nb0 = nb
        out_shape = jax.ShapeDtypeStruct((n, d), x.dtype)
        out_specs = pl.BlockSpec((tm, d), lambda i: (i, 0))
    else:
        nb0 = split_rows // tm
        out_shape = (jax.ShapeDtypeStruct((split_rows, d), x.dtype), jax.ShapeDtypeStruct((n - split_rows, d), x.dtype))
        out_specs = (pl.BlockSpec((tm, d), lambda i: (jnp.minimum(i, nb0 - 1), 0)),
                     pl.BlockSpec((tm, d), lambda i: (jnp.maximum(i - nb0, 0), 0)))
    return pl.pallas_call(
        functools.partial(_moe_combine_kernel, tm=tm, nb0=nb0),
        out_shape=out_shape,
        grid=(nb,),
        in_specs=[pl.BlockSpec(memory_space=pl.ANY), pl.BlockSpec(memory_space=pl.ANY),
                  pl.BlockSpec((tm, d), lambda i: (i, 0))],
        out_specs=out_specs,
        scratch_shapes=[pltpu.SMEM((2, 1, 2 * tm), jnp.int32),
                        pltpu.VMEM((2 * tm, d // 2), jnp.uint32),
                        pltpu.SemaphoreType.DMA((2,)), pltpu.SemaphoreType.DMA],
        compiler_params=_cparams(1, 8 * tm * d * 4 + (4 << 20)),
        name="moe_combine",
    )(pos_tiles, ys, x)


def _moe(x, gain, w_route, w_gate_up, w_down, layer, split_rows=None):
    n, d = x.shape
    n_exp = N_EXPERT_GROUPS * EXPERTS_PER_GROUP
    h, eid, ew = _norm_route(x, gain, w_route)
    flat_e = eid[:, :2].reshape(-1)
    flat_w = ew[:, :2].reshape(-1)
    nk = 2 * n
    onehot = (flat_e[:, None] == jnp.arange(n_exp, dtype=jnp.int32)[None, :]).astype(jnp.int32)
    csum = jnp.cumsum(onehot, axis=0)
    rank = jnp.take_along_axis(csum, flat_e[:, None], axis=1)[:, 0] - 1
    counts = csum[-1]
    pcounts = (counts + MOE_BLOCK - 1) // MOE_BLOCK * MOE_BLOCK
    pend = jnp.cumsum(pcounts)
    pstart = pend - pcounts
    dest = pstart[flat_e] + rank
    n_blocks = -(-nk // MOE_BLOCK) + n_exp
    n_slots = n_blocks * MOE_BLOCK
    flat_tok = jnp.arange(nk, dtype=jnp.int32) // 2
    slot_tab = jnp.zeros((n_slots, 2), jnp.int32).at[dest].set(
        jnp.stack([flat_tok, lax.bitcast_convert_type(flat_w, jnp.int32)], axis=1))
    slot_tok = slot_tab[:, 0]
    slot_w = lax.bitcast_convert_type(slot_tab[:, 1], jnp.float32)
    block_exp = jnp.minimum(jnp.searchsorted(pend, jnp.arange(n_blocks, dtype=jnp.int32) * MOE_BLOCK, side='right'),
                            n_exp - 1).astype(jnp.int32)
    n_used = (pend[-1] // MOE_BLOCK).astype(jnp.int32).reshape(1)
    xs = _row_gather(h, slot_tok, _tile(n_slots, _GATHER_ROWS))
    ys = _moe_experts(xs, slot_w, block_exp, n_used, w_gate_up, w_down, layer)
    return _moe_combine(x, ys, dest.reshape(n, 2), split_rows)


def _alibi_slopes(n):
    return [2.0 ** (-8.0 * (i + 1) / n) for i in range(n)]


def _token_mixers(x, l, p, n_seq):
    n, d = sum(a.shape[0] for a in _segments(x)), _segments(x)[0].shape[1]
    s_len = n // n_seq
    bf16 = jnp.bfloat16
    n_ga = len(DIL_GROUPS)
    a_w = A_HEADS * HEAD_DIM
    a_cols = n_ga * a_w
    b_qk = B_HEADS * 2 * B_QK_DIM
    b_w = B_HEADS * 2 * B_QK_DIM
    in_cols = p['w_in'].shape[2]
    slopes = _alibi_slopes(n_ga * A_HEADS + B_HEADS)

    h = _rmsnorm(x, p['norm_mix'][l])
    proj = _matmul(h, p['w_in'], l, bf16, name="in_proj")
    gates = _matmul(h, p['w_gate'], l, bf16, sigmoid=True, name="gate_proj")
    proj3 = proj.reshape(n_seq, s_len, in_cols)

    outs, lses = [], []
    for g, (_, dilation) in enumerate(DIL_GROUPS):
        o, lse = _attn_a_group(proj3, p['qnorm_a'][l], p['knorm_a'][l], group=g, dilation=dilation,
                               slopes=tuple(slopes[g * A_HEADS:(g + 1) * A_HEADS]), in_cols=in_cols,
                               n_heads=A_HEADS)
        outs.append(o.reshape(n, a_w))
        lses.append(lse.reshape(n, LANES))
    oa = _combine_a(outs, lses, A_HEADS)

    lam_init = 0.8 - 0.6 * math.exp(-0.3 * l)
    lam_vecs = jnp.stack([p['lambda_q1'][l], p['lambda_k1'][l], p['lambda_q2'][l], p['lambda_k2'][l]])
    ob = _attn_b(proj3, p['qnorm_b'][l], p['knorm_b'][l], lam_vecs, p['subln_b'][l],
                 slopes[n_ga * A_HEADS:], lam_init=lam_init,
                 q_col=3 * a_cols, k_col=3 * a_cols + b_qk, v_col=3 * a_cols + 2 * b_qk,
                 n_heads=B_HEADS).reshape(n, b_w)

    c0 = 3 * a_cols + 2 * b_qk + b_w
    oc = _conv(proj3, p['conv_w'][l], u_col=c0, b_col=c0 + C_WIDTH, c_col=c0 + 2 * C_WIDTH).reshape(n, C_WIDTH)

    merged = _gated_proj(oa, ob, oc, gates, p['w_proj_a'], p['w_proj_b'], p['w_proj_c'], l)
    return _matmul(merged, p['w_out'], l, jnp.float32, residual=x, tn=512, name="out_proj")


def _route_weights(w_group, w_expert):
    d = w_group.shape[0]
    used = w_group.shape[1] + w_expert.shape[1]
    return jnp.concatenate([w_group, w_expert, jnp.zeros((d, LANES - used), w_group.dtype)],
                           axis=1).astype(jnp.bfloat16)


def kernel(x_prompt, x_sample, norm_mix, w_in, qnorm_a, knorm_a, qnorm_b, knorm_b, lambda_q1, lambda_k1,
           lambda_q2, lambda_k2, subln_b, conv_w, w_proj_a, w_proj_b, w_proj_c, w_gate, w_out, norm_ffn,
           w_route_group, w_route_expert, w_gate_up, w_down):
    bf16 = jnp.bfloat16
    p = dict(norm_mix=norm_mix, w_in=w_in.astype(bf16), qnorm_a=qnorm_a, knorm_a=knorm_a, qnorm_b=qnorm_b,
             knorm_b=knorm_b, lambda_q1=lambda_q1, lambda_k1=lambda_k1, lambda_q2=lambda_q2, lambda_k2=lambda_k2,
             subln_b=subln_b, conv_w=conv_w, w_proj_a=w_proj_a.astype(bf16), w_proj_b=w_proj_b.astype(bf16),
             w_proj_c=w_proj_c.astype(bf16), w_gate=w_gate.astype(bf16), w_out=w_out.astype(bf16))
    w_gate_up_b, w_down_b = w_gate_up.astype(bf16), w_down.astype(bf16)
    bp, s_len, d = x_prompt.shape
    bs = x_sample.shape[0]
    assert x_sample.shape[1:] == (s_len, d)
    n_seq = bp + bs
    x = (x_prompt.reshape(bp * s_len, d), x_sample.reshape(bs * s_len, d))
    depth = norm_mix.shape[0]
    for l in range(depth):
        x = _token_mixers(x, l, p, n_seq)
        x = _moe(x, norm_ffn[l], _route_weights(w_route_group[l], w_route_expert[l]), w_gate_up_b, w_down_b, l,
                 split_rows=bp * s_len if l == depth - 1 else None)
    return (x[0].reshape(bp, s_len, d), x[1].reshape(bs, s_len, d))
```

```python
import functools
import math

import jax
import jax.numpy as jnp
from jax import lax
from jax.experimental import pallas as pl
from jax.experimental.pallas import tpu as pltpu

HEAD_DIM = 128
DIL_GROUPS = ((128, 1), (512, 4), (2048, 16))
A_HEADS = 8
B_HEADS = 8
B_QK_DIM = 64
C_WIDTH = 2048
N_EXPERT_GROUPS = 4
EXPERTS_PER_GROUP = 8
D_FF_EXPERT = 1024
MOE_BLOCK = 256
RMS_EPS = 1e-6
NEG_BIG = -1e30

LANES = 128
V7X_VMEM_LIMIT_BYTES = 56 * 1024 * 1024

A_HALF = DIL_GROUPS[0][0] // (2 * DIL_GROUPS[0][1])


def _cparams(n_grid_dims, vmem_bytes):
    return pltpu.CompilerParams(
        dimension_semantics=("arbitrary",) * n_grid_dims,
        vmem_limit_bytes=int(min(max(vmem_bytes, 16 * 1024 * 1024), V7X_VMEM_LIMIT_BYTES)),
    )


def _tile(n, pref):
    t = min(n, pref)
    while n % t:
        t //= 2
    return t


def _segments(x):
    return tuple(x) if isinstance(x, (tuple, list)) else (x,)


def _segment_specs(segs, tm, tn, row_axis, col_of):
    nb0 = segs[0].shape[0] // tm
    if len(segs) == 1:
        return [pl.BlockSpec((tm, tn), lambda *g: (g[row_axis], col_of(*g)))], nb0
    return [pl.BlockSpec((tm, tn), lambda *g: (jnp.minimum(g[row_axis], nb0 - 1), col_of(*g))),
            pl.BlockSpec((tm, tn), lambda *g: (jnp.maximum(g[row_axis] - nb0, 0), col_of(*g)))], nb0


def _pick_segment(refs, step, nb0):
    if len(refs) == 1:
        return refs[0][...]
    return jnp.where(step < nb0, refs[0][...], refs[1][...])


def _rmsnorm_kernel(*refs, nb0):
    *x_refs, g_ref, o_ref = refs
    x = _pick_segment(x_refs, pl.program_id(0), nb0)
    ms = jnp.mean(x * x, axis=-1, keepdims=True)
    o_ref[...] = (x * lax.rsqrt(ms + RMS_EPS) * g_ref[...]).astype(o_ref.dtype)


def _rmsnorm(x, gain):
    segs = _segments(x)
    n, d = sum(a.shape[0] for a in segs), segs[0].shape[1]
    tm = _tile(min(a.shape[0] for a in segs), 256)
    x_specs, nb0 = _segment_specs(segs, tm, d, 0, lambda i: 0)
    return pl.pallas_call(
        functools.partial(_rmsnorm_kernel, nb0=nb0),
        out_shape=jax.ShapeDtypeStruct((n, d), jnp.bfloat16),
        grid=(n // tm,),
        in_specs=x_specs + [pl.BlockSpec((1, d), lambda i: (0, 0))],
        out_specs=pl.BlockSpec((tm, d), lambda i: (i, 0)),
        compiler_params=_cparams(1, (2 * len(segs) + 2) * tm * d * 4),
        name="rmsnorm",
    )(*segs, gain.reshape(1, d))


def _matmul_kernel(x_ref, w_ref, *refs, sigmoid, res_nb0):
    *r_refs, o_ref = refs
    acc = jnp.dot(x_ref[...], w_ref[...], preferred_element_type=jnp.float32)
    if sigmoid:
        acc = 1.0 / (1.0 + jnp.exp(-acc))
    if r_refs:
        acc = acc + _pick_segment(r_refs, pl.program_id(1), res_nb0)
    o_ref[...] = acc.astype(o_ref.dtype)


def _matmul(x, w, layer, out_dtype, *, sigmoid=False, residual=None, tm=1024, tn=1024, name="matmul"):
    n, k = x.shape
    c = w.shape[2]
    tm, tn = _tile(n, tm), _tile(c, tn)
    in_specs = [pl.BlockSpec((tm, k), lambda j, i: (i, 0)),
                pl.BlockSpec((None, k, tn), lambda j, i: (layer, 0, j))]
    args = [x, w]
    out_bytes = jnp.dtype(out_dtype).itemsize
    vmem = 2 * (tm * k * 2 + k * tn * 2 + tm * tn * out_bytes) + 2 * tm * tn * 4
    res_nb0 = 0
    if residual is not None:
        segs = _segments(residual)
        r_specs, res_nb0 = _segment_specs(segs, tm, tn, 1, lambda j, i: j)
        in_specs += r_specs
        args += list(segs)
        vmem += 2 * len(segs) * tm * tn * 4
    return pl.pallas_call(
        functools.partial(_matmul_kernel, sigmoid=sigmoid, res_nb0=res_nb0),
        out_shape=jax.ShapeDtypeStruct((n, c), out_dtype),
        grid=(c // tn, n // tm),
        in_specs=in_specs,
        out_specs=pl.BlockSpec((tm, tn), lambda j, i: (i, j)),
        compiler_params=_cparams(2, vmem + (4 << 20)),
        name=name,
    )(*args)


def _head_rmsnorm(x, gain):
    ms = jnp.mean(x * x, axis=-1, keepdims=True)
    return x * lax.rsqrt(ms + RMS_EPS) * gain


def _attn_a_kernel(q_ref, kp_ref, km_ref, kn_ref, vp_ref, vm_ref, vn_ref, gq_ref, gk_ref,
                   o_ref, lse_ref, qn_scr, kn_scr, v_scr, *, dilation, slopes, tu, sub):
    half = A_HALF
    i = pl.program_id(2)
    n_i = pl.num_programs(2)
    n_heads = len(slopes)
    scale = HEAD_DIM ** -0.5

    for h in range(n_heads):
        cs = slice(h * HEAD_DIM, (h + 1) * HEAD_DIM)
        qn_scr[:, cs] = _head_rmsnorm(q_ref[:, cs].astype(jnp.float32), gq_ref[...]).astype(qn_scr.dtype)
        for off, ref, rows in ((0, kp_ref, half), (half, km_ref, tu), (half + tu, kn_ref, half)):
            kn_scr[off:off + rows, cs] = _head_rmsnorm(ref[:, cs].astype(jnp.float32),
                                                       gk_ref[...]).astype(kn_scr.dtype)
    v_scr[0:half, :] = vp_ref[...]
    v_scr[half:half + tu, :] = vm_ref[...]
    v_scr[half + tu:, :] = vn_ref[...]

    kw = sub + 2 * half
    ii = lax.broadcasted_iota(jnp.int32, (sub, kw), 0)
    jj = lax.broadcasted_iota(jnp.int32, (sub, kw), 1)
    rel = jnp.abs(jj - half - ii)
    band = rel <= half
    dist = (rel * dilation).astype(jnp.float32)
    lane = lax.broadcasted_iota(jnp.int32, (sub, LANES), 1)

    def body(t, carry):
        a = pl.multiple_of(t * sub, sub)
        kpos = a + jj
        valid = band & ((kpos >= half) | (i > 0)) & ((kpos < tu + half) | (i < n_i - 1))
        lse_tile = jnp.zeros((sub, LANES), jnp.float32)
        for h in range(n_heads):
            cs = slice(h * HEAD_DIM, (h + 1) * HEAD_DIM)
            qh = qn_scr[pl.ds(a, sub), cs]
            kh = kn_scr[pl.ds(a, kw), cs]
            vh = v_scr[pl.ds(a, kw), cs]
            s = lax.dot_general(qh, kh, (((1,), (1,)), ((), ())), preferred_element_type=jnp.float32)
            s = jnp.where(valid, s * scale - slopes[h] * dist, NEG_BIG)
            m = jnp.max(s, axis=-1, keepdims=True)
            p = jnp.exp(s - m)
            l = jnp.sum(p, axis=-1, keepdims=True)
            o = jnp.dot(p.astype(vh.dtype), vh, preferred_element_type=jnp.float32) / l
            o_ref[pl.ds(a, sub), cs] = o.astype(o_ref.dtype)
            lse_tile = jnp.where(lane == h, m + jnp.log(l), lse_tile)
        lse_ref[pl.ds(a, sub), :] = lse_tile
        return carry

    lax.fori_loop(0, tu // sub, body, 0)


def _deinterleave_kernel(*refs, dilation, width):
    *in_refs, o_ref, scr = refs
    per = in_refs[0].shape[0] // dilation
    n_lane_blocks = width // LANES
    for c, ref in enumerate(in_refs):
        for cc in range(n_lane_blocks):
            scr[cc] = ref[:, cc * LANES:(cc + 1) * LANES].astype(jnp.float32)
        for r in range(dilation):
            col = (r * len(in_refs) + c) * width
            for cc in range(n_lane_blocks):
                o_ref[:, col + cc * LANES:col + (cc + 1) * LANES] = (
                    scr[cc, pl.ds(r, per, stride=dilation), :].astype(o_ref.dtype))


def _deinterleave(proj3, col_blocks, width, dilation):
    n_seq, s_len, _ = proj3.shape
    rt = _tile(s_len, 1024)
    nc = len(col_blocks)
    return pl.pallas_call(
        functools.partial(_deinterleave_kernel, dilation=dilation, width=width),
        out_shape=jax.ShapeDtypeStruct((n_seq, s_len // dilation, dilation * nc * width), proj3.dtype),
        grid=(n_seq, s_len // rt),
        in_specs=[pl.BlockSpec((None, rt, width), lambda b, t, cb=cb: (b, t, cb)) for cb in col_blocks],
        out_specs=pl.BlockSpec((None, rt // dilation, dilation * nc * width), lambda b, t: (b, t, 0)),
        scratch_shapes=[pltpu.VMEM((width // LANES, rt, LANES), jnp.float32)],
        compiler_params=_cparams(2, V7X_VMEM_LIMIT_BYTES),
        name=f"deinterleave_d{dilation}",
    )(*([proj3] * nc))


def _attn_a_group(proj3, gq, gk, *, group, dilation, slopes, in_cols, n_heads):
    n_seq, s_len, _ = proj3.shape
    half = A_HALF
    u_len = s_len // dilation
    width = n_heads * HEAD_DIM
    a_cols = len(DIL_GROUPS) * width
    tu = _tile(u_len, 512)
    sub = min(tu, 128)
    q_blk = group
    k_blk = a_cols // width + group
    v_blk = 2 * a_cols // width + group
    if dilation > 1:
        pv = _deinterleave(proj3, (q_blk, k_blk, v_blk), width, dilation)
        in_cols, q_blk, k_blk, v_blk = 3 * width, 0, 1, 2
    else:
        pv = proj3
    blocks_per_row = in_cols // width
    nh = tu // half
    n_halo = u_len // half

    def main(col):
        return pl.BlockSpec((None, tu, width), lambda b, r, i: (b, i, r * blocks_per_row + col))

    def prev(col):
        return pl.BlockSpec((None, half, width),
                            lambda b, r, i: (b, jnp.maximum(i * nh - 1, 0), r * blocks_per_row + col))

    def nxt(col):
        return pl.BlockSpec((None, half, width),
                            lambda b, r, i: (b, jnp.minimum((i + 1) * nh, n_halo - 1), r * blocks_per_row + col))

    gspec = pl.BlockSpec((1, HEAD_DIM), lambda b, r, i: (0, 0))
    o, lse = pl.pallas_call(
        functools.partial(_attn_a_kernel, dilation=dilation, slopes=slopes, tu=tu, sub=sub),
        out_shape=(jax.ShapeDtypeStruct((n_seq, u_len, dilation * width), jnp.bfloat16),
                   jax.ShapeDtypeStruct((n_seq, u_len, dilation * LANES), jnp.float32)),
        grid=(n_seq, dilation, u_len // tu),
        in_specs=[main(q_blk), prev(k_blk), main(k_blk), nxt(k_blk),
                  prev(v_blk), main(v_blk), nxt(v_blk), gspec, gspec],
        out_specs=(pl.BlockSpec((None, tu, width), lambda b, r, i: (b, i, r)),
                   pl.BlockSpec((None, tu, LANES), lambda b, r, i: (b, i, r))),
        scratch_shapes=[pltpu.VMEM((tu, width), jnp.bfloat16),
                        pltpu.VMEM((tu + 2 * half, width), jnp.bfloat16),
                        pltpu.VMEM((tu + 2 * half, width), jnp.bfloat16)],
        compiler_params=_cparams(3, 32 << 20),
        name=f"attn_a_g{group}",
    )(pv, pv, pv, pv, pv, pv, pv, gq.reshape(1, HEAD_DIM), gk.reshape(1, HEAD_DIM))
    return o.reshape(n_seq, s_len, width), lse.reshape(n_seq, s_len, LANES)


def _combine_a_kernel(o0_ref, o1_ref, o2_ref, l0_ref, l1_ref, l2_ref, out_ref, *, n_heads):
    l0, l1, l2 = l0_ref[...], l1_ref[...], l2_ref[...]
    m = jnp.maximum(jnp.maximum(l0, l1), l2)
    e0, e1, e2 = jnp.exp(l0 - m), jnp.exp(l1 - m), jnp.exp(l2 - m)
    den = e0 + e1 + e2
    w0, w1, w2 = e0 / den, e1 / den, e2 / den
    for h in range(n_heads):
        cs = slice(h * HEAD_DIM, (h + 1) * HEAD_DIM)
        acc = (w0[:, h:h + 1] * o0_ref[:, cs].astype(jnp.float32)
               + w1[:, h:h + 1] * o1_ref[:, cs].astype(jnp.float32)
               + w2[:, h:h + 1] * o2_ref[:, cs].astype(jnp.float32))
        out_ref[:, cs] = acc.astype(out_ref.dtype)


def _combine_a(outs, lses, n_heads):
    n, width = outs[0].shape
    tm = _tile(n, 512)
    ospec = pl.BlockSpec((tm, width), lambda i: (i, 0))
    lspec = pl.BlockSpec((tm, LANES), lambda i: (i, 0))
    return pl.pallas_call(
        functools.partial(_combine_a_kernel, n_heads=n_heads),
        out_shape=jax.ShapeDtypeStruct((n, width), jnp.bfloat16),
        grid=(n // tm,),
        in_specs=[ospec, ospec, ospec, lspec, lspec, lspec],
        out_specs=ospec,
        compiler_params=_cparams(1, 24 << 20),
        name="combine_a",
    )(*outs, *lses)


_B_FEAT = 4
_B_PAD_ROWS = 16


def _split_hi_lo(x):
    hi = x.astype(jnp.bfloat16)
    lo = (x - hi.astype(jnp.float32)).astype(jnp.bfloat16)
    return hi.astype(jnp.float32), lo.astype(jnp.float32)


_LOG2E = 1.4426950408889634


def _attn_b_kernel(q_ref, k_ref, v_ref, gq_ref, gk_ref, lam_ref, gs_ref, slope_ref, o_ref,
                   k_scr, vt_scr, q_scr, s_scr, m_scr, acc_scr, *, lam_init, tq, tk, s_len):
    i = pl.program_id(2)
    dq = B_QK_DIM
    n_kt = s_len // tk
    slope = slope_ref[...] * _LOG2E
    slope1 = slope[:, 0:1]

    lane_q = lax.broadcasted_iota(jnp.int32, (tq, 2 * dq), 1)
    lane_k = lax.broadcasted_iota(jnp.int32, (tk, 2 * dq), 1)

    def norm_maps(x, gain, lane):
        in0 = lane < dq
        sq = x * x
        ms0 = jnp.sum(jnp.where(in0, sq, 0.0), axis=-1, keepdims=True) / dq
        ms1 = jnp.sum(jnp.where(in0, 0.0, sq), axis=-1, keepdims=True) / dq
        inv = jnp.where(in0, lax.rsqrt(ms0 + RMS_EPS), lax.rsqrt(ms1 + RMS_EPS))
        return x * inv * gain

    @pl.when(i == 0)
    def _():
        def kbody(t, carry):
            r0 = pl.multiple_of(t * tk, tk)
            kn = norm_maps(k_ref[pl.ds(r0, tk), :].astype(jnp.float32), gk_ref[...], lane_k)
            b = lax.broadcasted_iota(jnp.int32, (tk, 2 * dq), 0).astype(jnp.float32)
            l_hi, l_lo = _split_hi_lo(slope * b)
            r_hi, r_lo = _split_hi_lo(slope * (tk - 1 - b))
            for m in range(2):
                f0 = (1 - m) * dq
                feat = jnp.where(lane_k == f0, l_hi,
                       jnp.where(lane_k == f0 + 1, l_lo,
                       jnp.where(lane_k == f0 + 2, r_hi,
                       jnp.where(lane_k == f0 + 3, r_lo, 0.0))))
                own = (lane_k >= m * dq) & (lane_k < (m + 1) * dq)
                k_scr[m, pl.ds(r0, tk), :] = jnp.where(own, kn, feat).astype(k_scr.dtype)
            vt_scr[t, 0:2 * dq, :] = v_ref[pl.ds(r0, tk), :].astype(jnp.float32).T.astype(vt_scr.dtype)
            row = lax.broadcasted_iota(jnp.int32, (_B_PAD_ROWS, tk), 0)
            vt_scr[t, 2 * dq:, :] = jnp.where(row == 0, 1.0, 0.0).astype(vt_scr.dtype)
            return carry
        lax.fori_loop(0, n_kt, kbody, 0)

    qn = norm_maps(q_ref[...].astype(jnp.float32), gq_ref[...], lane_q) * (dq ** -0.5 * _LOG2E)
    for m in range(2):
        f0 = (1 - m) * dq
        own = (lane_q >= m * dq) & (lane_q < (m + 1) * dq)
        left = (lane_q == f0) | (lane_q == f0 + 1)
        right = (lane_q == f0 + 2) | (lane_q == f0 + 3)
        q_scr[m, 0] = jnp.where(own, qn, 0.0).astype(q_scr.dtype)
        q_scr[m, 1] = jnp.where(own, qn, jnp.where(left, 1.0, 0.0)).astype(q_scr.dtype)
        q_scr[m, 2] = jnp.where(own, qn, jnp.where(right, 1.0, 0.0)).astype(q_scr.dtype)

    m_scr[...] = jnp.full(m_scr.shape, NEG_BIG, jnp.float32)
    acc_scr[...] = jnp.zeros(acc_scr.shape, jnp.float32)

    q_pos = i * tq + lax.broadcasted_iota(jnp.int32, (1, tq), 1)

    i_kt = (i * tq) // tk

    def scores(j, s_ref):
        j = jnp.asarray(j, jnp.int32)
        r0 = pl.multiple_of(j * tk, tk)
        ver = jnp.where(j == i_kt, 0, jnp.where(j < i_kt, 1, 2))
        for m in range(2):
            s_ref[m] = lax.dot_general(k_scr[m, pl.ds(r0, tk), :], q_scr[m, ver], (((1,), (1,)), ((), ())),
                                       preferred_element_type=jnp.float32)

    def diagonal_bias(j, s_ref):
        @pl.when(j == i_kt)
        def _():
            a = lax.broadcasted_iota(jnp.int32, (tk, tq), 0)
            b = lax.broadcasted_iota(jnp.int32, (tk, tq), 1)
            bias = slope1 * jnp.abs(a - b - (i * tq - i_kt * tk)).astype(jnp.float32)
            for m in range(2):
                s_ref[m] = s_ref[m] - bias

    def softmax_pv(j, s_ref):
        j = jnp.asarray(j, jnp.int32)
        sgn = jnp.where(j < i_kt, 1, jnp.where(j > i_kt, -1, 0))
        off = jnp.where(j > i_kt, tk - 1, 0)
        shift = slope1 * (sgn * (j * tk - q_pos) - off).astype(jnp.float32)
        vt = vt_scr[j]
        for m in range(2):
            s = s_ref[m]
            m_old = m_scr[m]
            m_new = jnp.maximum(m_old, jnp.max(s, axis=0, keepdims=True) + shift)
            p = jnp.exp2(s - (m_new - shift))
            alpha = jnp.exp2(m_old - m_new)
            acc_scr[m] = alpha * acc_scr[m] + jnp.dot(vt, p.astype(vt.dtype),
                                                      preferred_element_type=jnp.float32)
            m_scr[m] = m_new

    s_a, s_b = s_scr.at[0], s_scr.at[1]
    scores(0, s_a)
    diagonal_bias(0, s_a)

    def pair(t, carry):
        j = 2 * t
        scores(j + 1, s_b)
        softmax_pv(j, s_a)
        diagonal_bias(j + 1, s_b)
        scores(j + 2, s_a)
        softmax_pv(j + 1, s_b)
        diagonal_bias(j + 2, s_a)
        return carry

    lax.fori_loop(0, n_kt // 2 - 1, pair, 0)
    scores(n_kt - 1, s_b)
    softmax_pv(n_kt - 2, s_a)
    diagonal_bias(n_kt - 1, s_b)
    softmax_pv(n_kt - 1, s_b)

    lam_v = lam_ref[...]
    lam = (jnp.exp(jnp.sum(lam_v[0:1] * lam_v[1:2], axis=-1, keepdims=True))
           - jnp.exp(jnp.sum(lam_v[2:3] * lam_v[3:4], axis=-1, keepdims=True)) + lam_init)
    hw = 2 * dq
    out = (acc_scr[0, 0:hw, :] * (1.0 / acc_scr[0, hw:hw + 1, :])
           - lam * (acc_scr[1, 0:hw, :] * (1.0 / acc_scr[1, hw:hw + 1, :])))
    ms = jnp.mean(out * out, axis=0, keepdims=True)
    out = out * lax.rsqrt(ms + RMS_EPS) * gs_ref[...] * (1.0 - lam_init)
    o_ref[...] = out.T.astype(o_ref.dtype)


def _attn_b(proj3, gq, gk, lam_vecs, subln, slopes, *, lam_init, q_col, k_col, v_col, n_heads):
    n_seq, s_len, _ = proj3.shape
    hw = 2 * B_QK_DIM
    tq, tk = _tile(s_len, 1024), _tile(s_len, 1024)
    assert (s_len // tk) % 2 == 0 and tk % tq == 0, "key tiles are pipelined in pairs; a query tile sits in one key tile"
    qb, kb, vb = q_col // hw, k_col // hw, v_col // hw
    gq2 = jnp.tile(gq, 2).reshape(1, hw)
    gk2 = jnp.tile(gk, 2).reshape(1, hw)
    slope_arr = jnp.broadcast_to(jnp.asarray(slopes, jnp.float32)[:, None, None], (n_heads, 1, LANES))
    small = lambda shape: pl.BlockSpec(shape, lambda b, h, i: (0, 0))
    return pl.pallas_call(
        functools.partial(_attn_b_kernel, lam_init=lam_init, tq=tq, tk=tk, s_len=s_len),
        out_shape=jax.ShapeDtypeStruct((n_seq, s_len, n_heads * hw), jnp.bfloat16),
        grid=(n_seq, n_heads, s_len // tq),
        in_specs=[pl.BlockSpec((None, tq, hw), lambda b, h, i: (b, i, qb + h)),
                  pl.BlockSpec((None, s_len, hw), lambda b, h, i: (b, 0, kb + h)),
                  pl.BlockSpec((None, s_len, hw), lambda b, h, i: (b, 0, vb + h)),
                  small((1, hw)), small((1, hw)), small((4, B_QK_DIM)), small((hw, 1)),
                  pl.BlockSpec((None, 1, LANES), lambda b, h, i: (h, 0, 0))],
        out_specs=pl.BlockSpec((None, tq, hw), lambda b, h, i: (b, i, h)),
        scratch_shapes=[pltpu.VMEM((2, s_len, hw), jnp.bfloat16),
                        pltpu.VMEM((s_len // tk, hw + _B_PAD_ROWS, tk), jnp.bfloat16),
                        pltpu.VMEM((2, 3, tq, hw), jnp.bfloat16),
                        pltpu.VMEM((2, 2, tk, tq), jnp.float32),
                        pltpu.VMEM((2, 1, tq), jnp.float32),
                        pltpu.VMEM((2, hw + _B_PAD_ROWS, tq), jnp.float32)],
        compiler_params=_cparams(3, V7X_VMEM_LIMIT_BYTES),
        name="attn_b",
    )(proj3, proj3, proj3, gq2, gk2, lam_vecs, subln.reshape(hw, 1), slope_arr)


_CONV_HALO = 16


def _conv_kernel(u_ref, b_ref, c_ref, up_ref, cp_ref, un_ref, cn_ref, w_ref, o_ref, *, tm):
    i = pl.program_id(1)
    n_i = pl.num_programs(1)
    f32 = jnp.float32
    v = c_ref[...].astype(f32) * u_ref[...].astype(f32)
    v_prev = (cp_ref[_CONV_HALO - 1:_CONV_HALO, :].astype(f32) * up_ref[_CONV_HALO - 1:_CONV_HALO, :].astype(f32))
    v_next = cn_ref[0:1, :].astype(f32) * un_ref[0:1, :].astype(f32)
    v_prev = jnp.where(i > 0, v_prev, 0.0)
    v_next = jnp.where(i < n_i - 1, v_next, 0.0)
    row = lax.broadcasted_iota(jnp.int32, v.shape, 0)
    down = jnp.where(row == 0, v_prev, pltpu.roll(v, 1, axis=0))
    up = jnp.where(row == tm - 1, v_next, pltpu.roll(v, tm - 1, axis=0))
    w = w_ref[...]
    y = down * w[0:1] + v * w[1:2] + up * w[2:3]
    o_ref[...] = (b_ref[...].astype(f32) * y).astype(o_ref.dtype)


def _conv(proj3, conv_w, *, u_col, b_col, c_col):
    n_seq, s_len, _ = proj3.shape
    cw = conv_w.shape[1]
    tm = _tile(s_len, 512)
    tc = _tile(cw, 1024)
    ncb = cw // tc
    nh = tm // _CONV_HALO
    n_halo = s_len // _CONV_HALO

    def main(col):
        return pl.BlockSpec((None, tm, tc), lambda b, i, c: (b, i, col // tc + c))

    def prev(col):
        return pl.BlockSpec((None, _CONV_HALO, tc),
                            lambda b, i, c: (b, jnp.maximum(i * nh - 1, 0), col // tc + c))

    def nxt(col):
        return pl.BlockSpec((None, _CONV_HALO, tc),
                            lambda b, i, c: (b, jnp.minimum((i + 1) * nh, n_halo - 1), col // tc + c))

    return pl.pallas_call(
        functools.partial(_conv_kernel, tm=tm),
        out_shape=jax.ShapeDtypeStruct((n_seq, s_len, cw), jnp.bfloat16),
        grid=(n_seq, s_len // tm, ncb),
        in_specs=[main(u_col), main(b_col), main(c_col), prev(u_col), prev(c_col), nxt(u_col), nxt(c_col),
                  pl.BlockSpec((3, tc), lambda b, i, c: (0, c))],
        out_specs=pl.BlockSpec((None, tm, tc), lambda b, i, c: (b, i, c)),
        compiler_params=_cparams(3, 32 << 20),
        name="short_conv",
    )(proj3, proj3, proj3, proj3, proj3, proj3, proj3, conv_w)


def _gated_proj_kernel(oa_ref, ob_ref, oc_ref, g0_ref, g1_ref, g2_ref, wa_ref, wb_ref, wc_ref, o_ref):
    f32 = jnp.float32
    acc = g0_ref[...].astype(f32) * jnp.dot(oa_ref[...], wa_ref[...], preferred_element_type=f32)
    acc += g1_ref[...].astype(f32) * jnp.dot(ob_ref[...], wb_ref[...], preferred_element_type=f32)
    acc += g2_ref[...].astype(f32) * jnp.dot(oc_ref[...], wc_ref[...], preferred_element_type=f32)
    o_ref[...] = acc.astype(o_ref.dtype)


def _gated_proj(oa, ob, oc, gates, wa, wb, wc, layer):
    n, d = oa.shape[0], wa.shape[2]
    tm, tn = _tile(n, 512), _tile(d, 1024)
    nj = d // tn
    act = lambda a: pl.BlockSpec((tm, a.shape[1]), lambda j, i: (i, 0))
    gate = lambda br: pl.BlockSpec((tm, tn), lambda j, i: (i, br * nj + j))
    wsp = lambda w: pl.BlockSpec((None, w.shape[1], tn), lambda j, i: (layer, 0, j))
    ka, kb, kc = oa.shape[1], ob.shape[1], oc.shape[1]
    vmem = 2 * 2 * (tm * (ka + kb + kc) + 3 * tm * tn + (ka + kb + kc) * tn + tm * tn) + 4 * tm * tn * 4
    return pl.pallas_call(
        _gated_proj_kernel,
        out_shape=jax.ShapeDtypeStruct((n, d), jnp.bfloat16),
        grid=(nj, n // tm),
        in_specs=[act(oa), act(ob), act(oc), gate(0), gate(1), gate(2), wsp(wa), wsp(wb), wsp(wc)],
        out_specs=pl.BlockSpec((tm, tn), lambda j, i: (i, j)),
        compiler_params=_cparams(2, vmem + (4 << 20)),
        name="gated_proj",
    )(oa, ob, oc, gates, gates, gates, wa, wb, wc)


def _pack_bf16_pairs(x):
    c = x.shape[1] // 2
    bits = lax.bitcast_convert_type(x.astype(jnp.bfloat16).astype(jnp.float32), jnp.uint32)
    return (bits[:, :c] & jnp.uint32(0xFFFF0000)) | (bits[:, c:] >> 16)


def _unpack_bf16_pairs(p):
    hi = lax.bitcast_convert_type(p & jnp.uint32(0xFFFF0000), jnp.float32)
    lo = lax.bitcast_convert_type(p << 16, jnp.float32)
    return hi, lo


def _norm_route_kernel(x_ref, g_ref, wr_ref, h_ref, eid_ref, ew_ref):
    x = x_ref[...]
    ms = jnp.mean(x * x, axis=-1, keepdims=True)
    h = x * lax.rsqrt(ms + RMS_EPS) * g_ref[...]
    h_ref[...] = _pack_bf16_pairs(h)
    logits = jnp.dot(h.astype(jnp.bfloat16), wr_ref[...], preferred_element_type=jnp.float32)
    ng, ne = N_EXPERT_GROUPS, EXPERTS_PER_GROUP
    lane = lax.broadcasted_iota(jnp.int32, logits.shape, 1)
    big = jnp.int32(1 << 20)
    is_g = lane < ng
    gl = jnp.where(is_g, logits, NEG_BIG)
    gmax = jnp.max(gl, axis=-1, keepdims=True)
    garg = jnp.min(jnp.where(is_g & (gl == gmax), lane, big), axis=-1, keepdims=True)
    g_w = 1.0 / jnp.sum(jnp.where(is_g, jnp.exp(gl - gmax), 0.0), axis=-1, keepdims=True)
    lo = ng + ne * garg
    sel = (lane >= lo) & (lane < lo + ne)
    el = jnp.where(sel, logits, NEG_BIG)
    m1 = jnp.max(el, axis=-1, keepdims=True)
    a1 = jnp.min(jnp.where(sel & (el == m1), lane, big), axis=-1, keepdims=True)
    el2 = jnp.where(lane == a1, NEG_BIG, el)
    m2 = jnp.max(el2, axis=-1, keepdims=True)
    a2 = jnp.min(jnp.where(sel & (lane != a1) & (el2 == m2), lane, big), axis=-1, keepdims=True)
    t = jnp.exp(m2 - m1)
    w1 = g_w / (1.0 + t)
    w2 = g_w * t / (1.0 + t)
    eid_ref[...] = jnp.where(lane == 0, a1 - ng, jnp.where(lane == 1, a2 - ng, 0))
    ew_ref[...] = jnp.where(lane == 0, w1, jnp.where(lane == 1, w2, 0.0))


def _norm_route(x, gain, w_route):
    n, d = x.shape
    tm = _tile(n, 256)
    return pl.pallas_call(
        _norm_route_kernel,
        out_shape=(jax.ShapeDtypeStruct((n, d // 2), jnp.uint32),
                   jax.ShapeDtypeStruct((n, LANES), jnp.int32),
                   jax.ShapeDtypeStruct((n, LANES), jnp.float32)),
        grid=(n // tm,),
        in_specs=[pl.BlockSpec((tm, d), lambda i: (i, 0)),
                  pl.BlockSpec((1, d), lambda i: (0, 0)),
                  pl.BlockSpec((d, LANES), lambda i: (0, 0))],
        out_specs=(pl.BlockSpec((tm, d // 2), lambda i: (i, 0)),
                   pl.BlockSpec((tm, LANES), lambda i: (i, 0)),
                   pl.BlockSpec((tm, LANES), lambda i: (i, 0))),
        compiler_params=_cparams(1, 6 * tm * d * 4 + (8 << 20)),
        name="norm_route",
    )(x, gain.reshape(1, d), w_route)


_DMA_ISSUE_UNROLL = 8
_GATHER_ROWS = 4 * MOE_BLOCK


def _fetch_step_indices(idx_hbm, idx_smem, sem_idx):
    b = pl.program_id(0)
    slot = b % 2

    def idx_copy(step, s):
        return pltpu.make_async_copy(idx_hbm.at[step], idx_smem.at[s], sem_idx.at[s])

    @pl.when(b == 0)
    def _():
        idx_copy(0, 0).start()

    idx_copy(b, slot).wait()

    @pl.when(b + 1 < pl.num_programs(0))
    def _():
        idx_copy(b + 1, 1 - slot).start()

    return slot


def _gather_rows(src_hbm, dst_ref, idx_smem, slot, sem_rows, n_rows):
    def start_group(g, carry):
        for k in range(_DMA_ISSUE_UNROLL):
            r = g * _DMA_ISSUE_UNROLL + k
            pltpu.make_async_copy(src_hbm.at[pl.ds(idx_smem[slot, 0, r], 1), :], dst_ref.at[pl.ds(r, 1), :],
                                  sem_rows).start(priority=k % 2)
        return carry

    lax.fori_loop(0, n_rows // _DMA_ISSUE_UNROLL, start_group, 0)
    pltpu.make_async_copy(src_hbm.at[pl.ds(0, n_rows), :], dst_ref, sem_rows).wait()


def _row_gather_kernel(idx_hbm, src_hbm, o_ref, idx_smem, sem_idx, sem_rows, *, rows):
    slot = _fetch_step_indices(idx_hbm, idx_smem, sem_idx)
    _gather_rows(src_hbm, o_ref, idx_smem, slot, sem_rows, rows)


def _row_gather(src, idx, rows):
    n_out = idx.shape[0]
    d = src.shape[1]
    nb = n_out // rows
    return pl.pallas_call(
        functools.partial(_row_gather_kernel, rows=rows),
        out_shape=jax.ShapeDtypeStruct((n_out, d), src.dtype),
        grid=(nb,),
        in_specs=[pl.BlockSpec(memory_space=pl.ANY), pl.BlockSpec(memory_space=pl.ANY)],
        out_specs=pl.BlockSpec((rows, d), lambda b: (b, 0)),
        scratch_shapes=[pltpu.SMEM((2, 1, rows), jnp.int32),
                        pltpu.SemaphoreType.DMA((2,)), pltpu.SemaphoreType.DMA],
        compiler_params=_cparams(1, 4 * rows * d * 4 + (4 << 20)),
        name="row_gather",
    )(idx.reshape(nb, 1, rows), src)


def _moe_up_kernel(be_ref, nb_ref, x_ref, w_ref, h_ref, *, f):
    b = pl.program_id(0)

    @pl.when(b < nb_ref[0])
    def _():
        x_hi, x_lo = _unpack_bf16_pairs(x_ref[...])
        half = x_hi.shape[1]
        gu = (jnp.dot(x_hi.astype(jnp.bfloat16), w_ref[:half, :], preferred_element_type=jnp.float32)
              + jnp.dot(x_lo.astype(jnp.bfloat16), w_ref[half:, :], preferred_element_type=jnp.float32))
        g, u = gu[:, :f], gu[:, f:]
        h_ref[...] = (g * (1.0 / (1.0 + jnp.exp(-g))) * u).astype(h_ref.dtype)

    @pl.when(b >= nb_ref[0])
    def _():
        h_ref[...] = jnp.zeros(h_ref.shape, h_ref.dtype)


def _moe_down_kernel(be_ref, nb_ref, h_ref, sw_ref, w_ref, y_ref):
    b = pl.program_id(0)

    @pl.when(b < nb_ref[0])
    def _():
        y = jnp.dot(h_ref[...], w_ref[...], preferred_element_type=jnp.float32)
        y_ref[...] = _pack_bf16_pairs(y * sw_ref[...])

    @pl.when(b >= nb_ref[0])
    def _():
        y_ref[...] = jnp.zeros(y_ref.shape, y_ref.dtype)


def _moe_experts(xs, slot_w, block_exp, n_blocks_used, w_gate_up, w_down, layer):
    n_slots, dh = xs.shape
    d = 2 * dh
    f = w_down.shape[2]
    nb = n_slots // MOE_BLOCK
    hmid = pl.pallas_call(
        functools.partial(_moe_up_kernel, f=f),
        out_shape=jax.ShapeDtypeStruct((n_slots, f), jnp.bfloat16),
        grid_spec=pltpu.PrefetchScalarGridSpec(
            num_scalar_prefetch=2, grid=(nb,),
            in_specs=[pl.BlockSpec((MOE_BLOCK, dh), lambda b, be, nu: (b, 0)),
                      pl.BlockSpec((None, None, d, 2 * f), lambda b, be, nu: (layer, be[b], 0, 0))],
            out_specs=pl.BlockSpec((MOE_BLOCK, f), lambda b, be, nu: (b, 0))),
        compiler_params=_cparams(1, 2 * (MOE_BLOCK * d * 4 + d * 2 * f * 2) + 3 * MOE_BLOCK * 2 * f * 4 + (4 << 20)),
        name="moe_up",
    )(block_exp, n_blocks_used, xs, w_gate_up)
    return pl.pallas_call(
        _moe_down_kernel,
        out_shape=jax.ShapeDtypeStruct((n_slots, dh), jnp.uint32),
        grid_spec=pltpu.PrefetchScalarGridSpec(
            num_scalar_prefetch=2, grid=(nb,),
            in_specs=[pl.BlockSpec((MOE_BLOCK, f), lambda b, be, nu: (b, 0)),
                      pl.BlockSpec((MOE_BLOCK, 1), lambda b, be, nu: (b, 0)),
                      pl.BlockSpec((None, None, f, d), lambda b, be, nu: (layer, be[b], 0, 0))],
            out_specs=pl.BlockSpec((MOE_BLOCK, dh), lambda b, be, nu: (b, 0))),
        compiler_params=_cparams(1, 2 * (f * d * 2 + MOE_BLOCK * d * 4) + 2 * MOE_BLOCK * d * 4 + (4 << 20)),
        name="moe_down",
    )(block_exp, n_blocks_used, hmid, slot_w.reshape(n_slots, 1), w_down)


def _moe_combine_kernel(pos_hbm, ys_hbm, x_ref, *refs, tm, nb0, with_norm):
    if with_norm:
        g_ref, o_ref, h_ref, idx_smem, ybuf, sem_idx, sem_rows = refs
        o_refs = [o_ref]
    else:
        *o_refs, idx_smem, ybuf, sem_idx, sem_rows = refs
    slot = _fetch_step_indices(pos_hbm, idx_smem, sem_idx)
    _gather_rows(ys_hbm, ybuf, idx_smem, slot, sem_rows, 2 * tm)
    a_hi, a_lo = _unpack_bf16_pairs(ybuf[0:tm, :])
    b_hi, b_lo = _unpack_bf16_pairs(ybuf[tm:2 * tm, :])
    half = a_hi.shape[1]
    out_hi = x_ref[:, :half] + (a_hi + b_hi)
    out_lo = x_ref[:, half:] + (a_lo + b_lo)
    if with_norm:
        ms = (jnp.sum(out_hi * out_hi, axis=-1, keepdims=True)
              + jnp.sum(out_lo * out_lo, axis=-1, keepdims=True)) * (1.0 / (2 * half))
        inv = lax.rsqrt(ms + RMS_EPS)
        h_ref[:, :half] = (out_hi * inv * g_ref[:, :half]).astype(h_ref.dtype)
        h_ref[:, half:] = (out_lo * inv * g_ref[:, half:]).astype(h_ref.dtype)
    if len(o_refs) == 1:
        o_refs[0][:, :half] = out_hi
        o_refs[0][:, half:] = out_lo
    else:
        for o_ref, mine in ((o_refs[0], pl.program_id(0) < nb0), (o_refs[1], pl.program_id(0) >= nb0)):
            @pl.when(mine)
            def _(o_ref=o_ref):
                o_ref[:, :half] = out_hi
                o_ref[:, half:] = out_lo


def _moe_combine(x, ys, pos, split_rows=None, next_gain=None):
    n, d = x.shape
    tm = _tile(n if split_rows is None else math.gcd(split_rows, n - split_rows), MOE_BLOCK)
    nb = n // tm
    pos_tiles = pos.reshape(nb, tm, 2).transpose(0, 2, 1).reshape(nb, 1, 2 * tm)
    in_specs = [pl.BlockSpec(memory_space=pl.ANY), pl.BlockSpec(memory_space=pl.ANY),
                pl.BlockSpec((tm, d), lambda i: (i, 0))]
    args = [pos_tiles, ys, x]
    if split_rows is None:
        nb0 = nb
        out_shape = jax.ShapeDtypeStruct((n, d), x.dtype)
        out_specs = pl.BlockSpec((tm, d), lambda i: (i, 0))
        if next_gain is not None:
            in_specs.append(pl.BlockSpec((1, d), lambda i: (0, 0)))
            args.append(next_gain.reshape(1, d))
            out_shape = (out_shape, jax.ShapeDtypeStruct((n, d), jnp.bfloat16))
            out_specs = (out_specs, pl.BlockSpec((tm, d), lambda i: (i, 0)))
    else:
        assert next_gain is None
        nb0 = split_rows // tm
        out_shape = (jax.ShapeDtypeStruct((split_rows, d), x.dtype), jax.ShapeDtypeStruct((n - split_rows, d), x.dtype))
        out_specs = (pl.BlockSpec((tm, d), lambda i: (jnp.minimum(i, nb0 - 1), 0)),
                     pl.BlockSpec((tm, d), lambda i: (jnp.maximum(i - nb0, 0), 0)))
    return pl.pallas_call(
        functools.partial(_moe_combine_kernel, tm=tm, nb0=nb0, with_norm=next_gain is not None),
        out_shape=out_shape,
        grid=(nb,),
        in_specs=in_specs,
        out_specs=out_specs,
        scratch_shapes=[pltpu.SMEM((2, 1, 2 * tm), jnp.int32),
                        pltpu.VMEM((2 * tm, d // 2), jnp.uint32),
                        pltpu.SemaphoreType.DMA((2,)), pltpu.SemaphoreType.DMA],
        compiler_params=_cparams(1, 8 * tm * d * 4 + (4 << 20)),
        name="moe_combine",
    )(*args)


def _moe(x, gain, w_route, w_gate_up, w_down, layer, split_rows=None, next_gain=None):
    n, d = x.shape
    n_exp = N_EXPERT_GROUPS * EXPERTS_PER_GROUP
    h, eid, ew = _norm_route(x, gain, w_route)
    flat_e = eid[:, :2].reshape(-1)
    flat_w = ew[:, :2].reshape(-1)
    nk = 2 * n
    onehot = (flat_e[:, None] == jnp.arange(n_exp, dtype=jnp.int32)[None, :]).astype(jnp.int32)
    csum = jnp.cumsum(onehot, axis=0)
    rank = jnp.take_along_axis(csum, flat_e[:, None], axis=1)[:, 0] - 1
    counts = csum[-1]
    pcounts = (counts + MOE_BLOCK - 1) // MOE_BLOCK * MOE_BLOCK
    pend = jnp.cumsum(pcounts)
    pstart = pend - pcounts
    dest = pstart[flat_e] + rank
    n_blocks = -(-nk // MOE_BLOCK) + n_exp
    n_slots = n_blocks * MOE_BLOCK
    flat_tok = jnp.arange(nk, dtype=jnp.int32) // 2
    slot_tab = jnp.zeros((n_slots, 2), jnp.int32).at[dest].set(
        jnp.stack([flat_tok, lax.bitcast_convert_type(flat_w, jnp.int32)], axis=1))
    slot_tok = slot_tab[:, 0]
    slot_w = lax.bitcast_convert_type(slot_tab[:, 1], jnp.float32)
    block_exp = jnp.minimum(jnp.searchsorted(pend, jnp.arange(n_blocks, dtype=jnp.int32) * MOE_BLOCK, side='right'),
                            n_exp - 1).astype(jnp.int32)
    n_used = (pend[-1] // MOE_BLOCK).astype(jnp.int32).reshape(1)
    xs = _row_gather(h, slot_tok, _tile(n_slots, _GATHER_ROWS))
    ys = _moe_experts(xs, slot_w, block_exp, n_used, w_gate_up, w_down, layer)
    return _moe_combine(x, ys, dest.reshape(n, 2), split_rows, next_gain)


def _alibi_slopes(n):
    return [2.0 ** (-8.0 * (i + 1) / n) for i in range(n)]


def _token_mixers(x, l, p, n_seq, h=None):
    n, d = sum(a.shape[0] for a in _segments(x)), _segments(x)[0].shape[1]
    s_len = n // n_seq
    bf16 = jnp.bfloat16
    n_ga = len(DIL_GROUPS)
    a_w = A_HEADS * HEAD_DIM
    a_cols = n_ga * a_w
    b_qk = B_HEADS * 2 * B_QK_DIM
    b_w = B_HEADS * 2 * B_QK_DIM
    in_cols = p['w_in'].shape[2]
    slopes = _alibi_slopes(n_ga * A_HEADS + B_HEADS)

    if h is None:
        h = _rmsnorm(x, p['norm_mix'][l])
    proj = _matmul(h, p['w_in'], l, bf16, name="in_proj")
    gates = _matmul(h, p['w_gate'], l, bf16, sigmoid=True, name="gate_proj")
    proj3 = proj.reshape(n_seq, s_len, in_cols)

    outs, lses = [], []
    for g, (_, dilation) in enumerate(DIL_GROUPS):
        o, lse = _attn_a_group(proj3, p['qnorm_a'][l], p['knorm_a'][l], group=g, dilation=dilation,
                               slopes=tuple(slopes[g * A_HEADS:(g + 1) * A_HEADS]), in_cols=in_cols,
                               n_heads=A_HEADS)
        outs.append(o.reshape(n, a_w))
        lses.append(lse.reshape(n, LANES))
    oa = _combine_a(outs, lses, A_HEADS)

    lam_init = 0.8 - 0.6 * math.exp(-0.3 * l)
    lam_vecs = jnp.stack([p['lambda_q1'][l], p['lambda_k1'][l], p['lambda_q2'][l], p['lambda_k2'][l]])
    ob = _attn_b(proj3, p['qnorm_b'][l], p['knorm_b'][l], lam_vecs, p['subln_b'][l],
                 slopes[n_ga * A_HEADS:], lam_init=lam_init,
                 q_col=3 * a_cols, k_col=3 * a_cols + b_qk, v_col=3 * a_cols + 2 * b_qk,
                 n_heads=B_HEADS).reshape(n, b_w)

    c0 = 3 * a_cols + 2 * b_qk + b_w
    oc = _conv(proj3, p['conv_w'][l], u_col=c0, b_col=c0 + C_WIDTH, c_col=c0 + 2 * C_WIDTH).reshape(n, C_WIDTH)

    merged = _gated_proj(oa, ob, oc, gates, p['w_proj_a'], p['w_proj_b'], p['w_proj_c'], l)
    return _matmul(merged, p['w_out'], l, jnp.float32, residual=x, tn=512, name="out_proj")


def _route_weights(w_group, w_expert):
    d = w_group.shape[0]
    used = w_group.shape[1] + w_expert.shape[1]
    return jnp.concatenate([w_group, w_expert, jnp.zeros((d, LANES - used), w_group.dtype)],
                           axis=1).astype(jnp.bfloat16)


def kernel(x_prompt, x_sample, norm_mix, w_in, qnorm_a, knorm_a, qnorm_b, knorm_b, lambda_q1, lambda_k1,
           lambda_q2, lambda_k2, subln_b, conv_w, w_proj_a, w_proj_b, w_proj_c, w_gate, w_out, norm_ffn,
           w_route_group, w_route_expert, w_gate_up, w_down):
    bf16 = jnp.bfloat16
    p = dict(norm_mix=norm_mix, w_in=w_in.astype(bf16), qnorm_a=qnorm_a, knorm_a=knorm_a, qnorm_b=qnorm_b,
             knorm_b=knorm_b, lambda_q1=lambda_q1, lambda_k1=lambda_k1, lambda_q2=lambda_q2, lambda_k2=lambda_k2,
             subln_b=subln_b, conv_w=conv_w, w_proj_a=w_proj_a.astype(bf16), w_proj_b=w_proj_b.astype(bf16),
             w_proj_c=w_proj_c.astype(bf16), w_gate=w_gate.astype(bf16), w_out=w_out.astype(bf16))
    w_gate_up_b, w_down_b = w_gate_up.astype(bf16), w_down.astype(bf16)
    bp, s_len, d = x_prompt.shape
    bs = x_sample.shape[0]
    assert x_sample.shape[1:] == (s_len, d)
    n_seq = bp + bs
    x = (x_prompt.reshape(bp * s_len, d), x_sample.reshape(bs * s_len, d))
    depth = norm_mix.shape[0]
    h = None
    for l in range(depth):
        last = l == depth - 1
        x = _token_mixers(x, l, p, n_seq, h)
        x = _moe(x, norm_ffn[l], _route_weights(w_route_group[l], w_route_expert[l]), w_gate_up_b, w_down_b, l,
                 split_rows=bp * s_len if last else None, next_gain=None if last else norm_mix[l + 1])
        if not last:
            x, h = x
    return (x[0].reshape(bp, s_len, d), x[1].reshape(bs, s_len, d))
```
